```python
import math
import jax, jax.numpy as jnp
from jax import lax
import numpy as np

D_MODEL = 1024
BATCH = 32
SEQ = 2048
DEPTH = 2

CHUNK = 64
HEAD_DIM = 64
N_MIX_HEADS = D_MODEL // HEAD_DIM
ATT_HEADS = N_MIX_HEADS // 4
RWKV_HEADS = (N_MIX_HEADS - ATT_HEADS) // 2
RET_HEADS = N_MIX_HEADS - ATT_HEADS - RWKV_HEADS
RWKV_W = RWKV_HEADS * HEAD_DIM
RET_W = RET_HEADS * HEAD_DIM
ATT_W = ATT_HEADS * HEAD_DIM
D_MIX = RWKV_W + RET_W + ATT_W

DECAY_LORA = 64
AAA_LORA = 64
GATE_LORA = 128
RWKV_SPLITS = (RWKV_W, RWKV_W, RWKV_W, DECAY_LORA, AAA_LORA, GATE_LORA)
RWKV_COLS = sum(RWKV_SPLITS)
RET_COLS = 4 * RET_W
ATT_COLS = 3 * ATT_W
N_IN_COLS = RWKV_COLS + RET_COLS + ATT_COLS

BAND_PREV_CHUNKS = 8
BAND = (BAND_PREV_CHUNKS + 1) * CHUNK
REL_CLIP = 128
REL_TABLE = (CHUNK - 1) + REL_CLIP + 1

D_FF = 256 * ((8 * D_MODEL // 3 + 255) // 256)
CONV_W = 3

ALPHA = (2 * DEPTH) ** 0.25
BETA = (8 * DEPTH) ** -0.25
ROPE_BASE = 10000.0
LN_EPS = 1e-5
RWKV_GN_EPS = 64e-5
RET_GN_EPS = 1e-5

kernel_name = "hybrid_rwkv7_retnet_chunkattn_deepnorm"


def _offsets(sizes):
    out, acc = [], 0
    for s in sizes[:-1]:
        acc += s
        out.append(acc)
    return out


def layer_norm(x, g, b, eps=LN_EPS):
    xf = x.astype(jnp.float32)
    mu = jnp.mean(xf, axis=-1, keepdims=True)
    var = jnp.mean(jnp.square(xf - mu), axis=-1, keepdims=True)
    y = (xf - mu) * lax.rsqrt(var + eps) * g.astype(jnp.float32) + b.astype(jnp.float32)
    return y.astype(x.dtype)


def head_norm(y, g, b, eps):
    H, d = y.shape[-2], y.shape[-1]
    mu = jnp.mean(y, axis=-1, keepdims=True)
    var = jnp.mean(jnp.square(y - mu), axis=-1, keepdims=True)
    yn = (y - mu) * lax.rsqrt(var + eps)
    return yn * g.astype(jnp.float32).reshape(H, d) + b.astype(jnp.float32).reshape(H, d)


def token_shift(z, mu):
    prev = jnp.pad(z, ((0, 0), (1, 0), (0, 0)))[:, :-1]
    return z + (prev - z) * mu


def rope(t, pos):
    half = t.shape[-1] // 2
    inv = ROPE_BASE ** (-jnp.arange(half, dtype=jnp.float32) / half)
    ang = pos[:, None] * inv[None, :]
    cos = jnp.cos(ang)[:, None, :]
    sin = jnp.sin(ang)[:, None, :]
    t1, t2 = t[..., :half], t[..., half:]
    return jnp.concatenate([t1 * cos - t2 * sin, t1 * sin + t2 * cos], axis=-1)


def rwkv7_time_mix(z, mu, w0, w_up, a0, a_up, g_up, k_k, k_a, r_k, ln_g, ln_b):
    B, S, _ = z.shape
    H, N = RWKV_HEADS, HEAD_DIM
    f32 = jnp.float32
    z = token_shift(z.astype(f32), mu.astype(f32))
    r, k, v, wl, al, gl = jnp.split(z, _offsets(RWKV_SPLITS), axis=-1)
    w_raw = -jax.nn.softplus(-(w0.astype(f32) + jnp.tanh(wl) @ w_up.astype(f32))) - 0.5
    decay = jnp.exp(-jnp.exp(w_raw))
    a = jax.nn.sigmoid(a0.astype(f32) + al @ a_up.astype(f32))
    g = jax.nn.sigmoid(gl) @ g_up.astype(f32)

    def heads(t):
        return t.reshape(B, S, H, N)

    kk = heads(k * k_k.astype(f32))
    kk = kk / jnp.maximum(jnp.sqrt(jnp.sum(kk * kk, axis=-1, keepdims=True)), 1e-12)
    k = heads(k * (1.0 + (a - 1.0) * k_a.astype(f32)))
    r, v, decay, a = heads(r), heads(v), heads(decay), heads(a)
    a_vec = -kk
    b_vec = kk * a

    def step(state, inp):
        r_t, w_t, k_t, v_t, a_t, b_t = inp
        sa = jnp.einsum('bhvk,bhk->bhv', state, a_t)
        state = (state * w_t[:, :, None, :] + sa[..., None] * b_t[:, :, None, :]
                 + v_t[..., None] * k_t[:, :, None, :])
        return state, jnp.einsum('bhvk,bhk->bhv', state, r_t)

    xs = tuple(jnp.moveaxis(t, 1, 0) for t in (r, decay, k, v, a_vec, b_vec))
    _, y = lax.scan(step, jnp.zeros((B, H, N, N), f32), xs)
    y = jnp.moveaxis(y, 0, 1)
    bonus = jnp.sum(r * k * r_k.astype(f32), axis=-1, keepdims=True) * v
    y = head_norm(y, ln_g, ln_b, RWKV_GN_EPS) + bonus
    return y.reshape(B, S, RWKV_W) * g


def retention_mix(z, gn_g, gn_b):
    B, S, _ = z.shape
    H, d, C = RET_HEADS, HEAD_DIM, CHUNK
    nc = S // C
    f32 = jnp.float32
    q, k, v, g = jnp.split(z.astype(f32), [RET_W, 2 * RET_W, 3 * RET_W], axis=-1)
    pos = jnp.arange(S, dtype=f32)
    q = rope(q.reshape(B, S, H, d), pos)
    k = rope(k.reshape(B, S, H, d), pos) * (d ** -0.5)
    v = v.reshape(B, S, H, d)
    log_g = jnp.log(1.0 - jnp.exp2(-5.0 - jnp.arange(H, dtype=f32)))
    q = q.reshape(B, nc, C, H, d)
    k = k.reshape(B, nc, C, H, d)
    v = v.reshape(B, nc, C, H, d)
    idx = jnp.arange(C, dtype=f32)
    dmat = jnp.exp(log_g[:, None, None] * jnp.abs(idx[:, None] - idx[None, :]))
    scores = jnp.einsum('bnihd,bnjhd->bnhij', q, k) * dmat
    intra = jnp.einsum('bnhij,bnjhe->bnihe', scores, v)
    k_dec = jnp.exp(log_g[None, :] * (C - 1.0 - idx)[:, None])
    u = jnp.einsum('bnjhd,bnjhe->nbhde', k * k_dec[None, None, :, :, None], v)
    chunk_decay = jnp.exp(log_g * C)[None, :, None, None]

    def step(state, u_n):
        return state * chunk_decay + u_n, state

    _, r_prev = lax.scan(step, jnp.zeros((B, H, d, d), f32), u)
    q_dec = jnp.exp(log_g[None, :] * (idx + 1.0)[:, None])
    inter = jnp.einsum('bnihd,nbhde->bnihe', q, r_prev) * q_dec[None, None, :, :, None]
    y = (intra + inter).reshape(B, S, H, d)
    y = head_norm(y, gn_g, gn_b, RET_GN_EPS).reshape(B, S, RET_W)
    return y * jax.nn.silu(g)


def chunk_band_attention(z, rel_bias):
    B, S, _ = z.shape
    H, d, C = ATT_HEADS, HEAD_DIM, CHUNK
    nc = S // C
    pad = BAND_PREV_CHUNKS * C
    f32 = jnp.float32
    q, k, v = jnp.split(z.astype(f32), [ATT_W, 2 * ATT_W], axis=-1)
    q = q.reshape(B, S, H, d) * (d ** -0.5)
    k_pad = jnp.pad(k.reshape(B, S, H, d), ((0, 0), (pad, 0), (0, 0), (0, 0)))
    v_pad = jnp.pad(v.reshape(B, S, H, d), ((0, 0), (pad, 0), (0, 0), (0, 0)))
    i = jnp.arange(C)
    j = jnp.arange(BAND)
    rel = (i[:, None] + pad) - j[None, :]
    rel_idx = jnp.clip(rel, -(C - 1), REL_CLIP) + (C - 1)
    bias = rel_bias.astype(f32)[:, rel_idx]

    def one_chunk(n):
        start = n * C
        q_n = lax.dynamic_slice_in_dim(q, start, C, axis=1)
        k_n = lax.dynamic_slice_in_dim(k_pad, start, BAND, axis=1)
        v_n = lax.dynamic_slice_in_dim(v_pad, start, BAND, axis=1)
        s = jnp.einsum('bihd,bjhd->bhij', q_n, k_n) + bias[None]
        valid = (start + j) >= pad
        s = jnp.where(valid[None, None, None, :], s, -1e30)
        p = jax.nn.softmax(s, axis=-1)
        return jnp.einsum('bhij,bjhd->bihd', p, v_n)

    out = lax.map(one_chunk, jnp.arange(nc))
    return jnp.moveaxis(out, 0, 1).reshape(B, S, ATT_W)


def conv_gated_ffn(x, w_up, conv_w, conv_b, w_down):
    S = x.shape[1]
    u = x @ w_up
    u_pad = jnp.pad(u, ((0, 0), (CONV_W - 1, 0), (0, 0)))
    c = conv_b
    for t in range(CONV_W):
        c = c + conv_w[t] * u_pad[:, t:t + S]
    gate, val = jnp.split(c, 2, axis=-1)
    return (jax.nn.silu(gate) * val) @ w_down


def setup_inputs(seed: int = 0) -> dict:
    key = jax.random.key(seed)
    ks = jax.random.split(key, 32)
    f32 = jnp.float32
    nrm = lambda k, s: jax.random.normal(k, s, f32)
    L = DEPTH
    return {
        "x": nrm(ks[0], (BATCH, SEQ, D_MODEL)),
        "ln_in_g": 1.0 + 0.02 * nrm(ks[1], (D_MODEL,)),
        "ln_in_b": 0.02 * nrm(ks[2], (D_MODEL,)),
        "w_in": nrm(ks[3], (L, D_MODEL, N_IN_COLS)) * D_MODEL ** -0.5,
        "rw_mu": jax.random.uniform(ks[4], (L, RWKV_COLS), f32),
        "rw_w0": jax.random.uniform(ks[5], (L, RWKV_W), f32, minval=-6.0, maxval=-1.0),
        "rw_w_up": nrm(ks[6], (L, DECAY_LORA, RWKV_W)) * 0.5 * DECAY_LORA ** -0.5,
        "rw_a0": 0.1 * nrm(ks[7], (L, RWKV_W)),
        "rw_a_up": nrm(ks[8], (L, AAA_LORA, RWKV_W)) * AAA_LORA ** -0.5,
        "rw_g_up": nrm(ks[9], (L, GATE_LORA, RWKV_W)) * GATE_LORA ** -0.5,
        "rw_k_k": 0.85 + 0.05 * nrm(ks[10], (L, RWKV_W)),
        "rw_k_a": 1.0 + 0.05 * nrm(ks[11], (L, RWKV_W)),
        "rw_r_k": 0.1 * nrm(ks[12], (L, RWKV_HEADS, HEAD_DIM)),
        "rw_ln_g": 1.0 + 0.02 * nrm(ks[13], (L, RWKV_W)),
        "rw_ln_b": 0.02 * nrm(ks[14], (L, RWKV_W)),
        "ret_gn_g": 1.0 + 0.02 * nrm(ks[15], (L, RET_W)),
        "ret_gn_b": 0.02 * nrm(ks[16], (L, RET_W)),
        "attn_rel_bias": 0.1 * nrm(ks[17], (L, ATT_HEADS, REL_TABLE)),
        "w_out": nrm(ks[18], (L, D_MIX, D_MODEL)) * D_MIX ** -0.5 * BETA,
        "ln1_g": 1.0 + 0.02 * nrm(ks[19], (L, D_MODEL)),
        "ln1_b": 0.02 * nrm(ks[20], (L, D_MODEL)),
        "ffn_w_up": nrm(ks[21], (L, D_MODEL, 2 * D_FF)) * D_MODEL ** -0.5,
        "ffn_conv_w": nrm(ks[22], (L, CONV_W, 2 * D_FF)) * CONV_W ** -0.5,
        "ffn_conv_b": 0.02 * nrm(ks[23], (L, 2 * D_FF)),
        "ffn_w_down": nrm(ks[24], (L, D_FF, D_MODEL)) * D_FF ** -0.5 * BETA,
        "ln2_g": 1.0 + 0.02 * nrm(ks[25], (L, D_MODEL)),
        "ln2_b": 0.02 * nrm(ks[26], (L, D_MODEL)),
    }


def reference(x, ln_in_g, ln_in_b, w_in, rw_mu, rw_w0, rw_w_up, rw_a0, rw_a_up, rw_g_up,
              rw_k_k, rw_k_a, rw_r_k, rw_ln_g, rw_ln_b, ret_gn_g, ret_gn_b, attn_rel_bias,
              w_out, ln1_g, ln1_b, ffn_w_up, ffn_conv_w, ffn_conv_b, ffn_w_down, ln2_g, ln2_b):
    x = layer_norm(x, ln_in_g, ln_in_b)
    for l in range(DEPTH):
        z = x @ w_in[l]
        z_rwkv = z[..., :RWKV_COLS]
        z_ret = z[..., RWKV_COLS:RWKV_COLS + RET_COLS]
        z_att = z[..., RWKV_COLS + RET_COLS:]
        y_rwkv = rwkv7_time_mix(z_rwkv, rw_mu[l], rw_w0[l], rw_w_up[l], rw_a0[l], rw_a_up[l],
                                rw_g_up[l], rw_k_k[l], rw_k_a[l], rw_r_k[l], rw_ln_g[l], rw_ln_b[l])
        y_ret = retention_mix(z_ret, ret_gn_g[l], ret_gn_b[l])
        y_att = chunk_band_attention(z_att, attn_rel_bias[l])
        y = jnp.concatenate([y_rwkv, y_ret, y_att], axis=-1).astype(x.dtype)
        x = layer_norm(ALPHA * x + y @ w_out[l], ln1_g[l], ln1_b[l])
        f = conv_gated_ffn(x, ffn_w_up[l], ffn_conv_w[l], ffn_conv_b[l], ffn_w_down[l])
        x = layer_norm(ALPHA * x + f.astype(x.dtype), ln2_g[l], ln2_b[l])
    return x
```

```python
import functools

import jax
import jax.numpy as jnp
from jax import lax
from jax.experimental import pallas as pl
from jax.experimental.pallas import tpu as pltpu

F32 = jnp.float32
BF16 = jnp.bfloat16

D_MODEL = 1024
DEPTH = 2
CHUNK = 64
HEAD_DIM = 64
PAIR = 2 * HEAD_DIM
RWKV_HEADS, RET_HEADS, ATT_HEADS = 6, 6, 4
RWKV_W, RET_W, ATT_W = 384, 384, 256
DECAY_LORA, AAA_LORA, GATE_LORA = 64, 64, 128
RWKV_COLS = 3 * RWKV_W + DECAY_LORA + AAA_LORA + GATE_LORA
RET_COLS = 4 * RET_W
ATT_COLS = 3 * ATT_W
N_IN_COLS = RWKV_COLS + RET_COLS + ATT_COLS
RET_OFF = RWKV_COLS
ATT_OFF = RWKV_COLS + RET_COLS
BAND_PREV_CHUNKS = 8
BAND_PAD = BAND_PREV_CHUNKS * CHUNK
BAND = BAND_PAD + CHUNK
REL_CLIP = 128
D_FF = 2816
CONV_W = 3
ALPHA = (2 * DEPTH) ** 0.25
ROPE_BASE = 10000.0
LN_EPS = 1e-5
RWKV_GN_EPS = 64e-5
RET_GN_EPS = 1e-5

ROW_TILE = 512
FFN_COLS = 256
VMEM_LIMIT = 56 * 1024 * 1024


def _dot(a, b):
    return jnp.dot(a.astype(BF16), b.astype(BF16), preferred_element_type=F32)


def _dot_nt(a, b):
    return lax.dot_general(a.astype(BF16), b.astype(BF16), (((1,), (1,)), ((), ())),
                           preferred_element_type=F32)


def _dot_tn(a, b):
    return lax.dot_general(a.astype(BF16), b.astype(BF16), (((0,), (0,)), ((), ())),
                           preferred_element_type=F32)


def _split2(x):
    hi = x.astype(BF16)
    lo = (x - hi.astype(F32)).astype(BF16)
    return hi, lo


def _split3(x):
    hi = x.astype(BF16)
    r1 = x - hi.astype(F32)
    mid = r1.astype(BF16)
    lo = (r1 - mid.astype(F32)).astype(BF16)
    return hi, mid, lo


def _dot_x3(x, w_hi, w_lo):
    x_hi, x_lo = _split2(x)
    return (jnp.dot(x_hi, w_hi, preferred_element_type=F32)
            + jnp.dot(x_hi, w_lo, preferred_element_type=F32)
            + jnp.dot(x_lo, w_hi, preferred_element_type=F32))


def _layer_norm(x, g, b, eps=LN_EPS):
    mu = jnp.mean(x, axis=-1, keepdims=True)
    d = x - mu
    var = jnp.mean(d * d, axis=-1, keepdims=True)
    return d * lax.rsqrt(var + eps) * g + b


def _sigmoid(x):
    return 1.0 / (1.0 + jnp.exp(-x))


def _proj_kernel(apply_ln, x_ref, g_ref, b_ref, w_ref, z_ref, *xn_ref):
    x = x_ref[...]
    if apply_ln:
        x = _layer_norm(x, g_ref[...], b_ref[...])
        xn_ref[0][...] = x
    z_ref[...] = jnp.dot(x.astype(BF16), w_ref[...], preferred_element_type=F32)


def _project(x2d, ln_g, ln_b, w_bf16, apply_ln):
    m = x2d.shape[0]
    tm = ROW_TILE
    row = lambda i: (i, 0)
    const = lambda i: (0, 0)
    out_shape = [jax.ShapeDtypeStruct((m, N_IN_COLS), F32)]
    out_specs = [pl.BlockSpec((tm, N_IN_COLS), row)]
    if apply_ln:
        out_shape.append(jax.ShapeDtypeStruct((m, D_MODEL), F32))
        out_specs.append(pl.BlockSpec((tm, D_MODEL), row))
    outs = pl.pallas_call(
        functools.partial(_proj_kernel, apply_ln),
        grid=(m // tm,),
        in_specs=[pl.BlockSpec((tm, D_MODEL), row),
                  pl.BlockSpec((1, D_MODEL), const),
                  pl.BlockSpec((1, D_MODEL), const),
                  pl.BlockSpec((D_MODEL, N_IN_COLS), const, pipeline_mode=pl.Buffered(1))],
        out_specs=out_specs,
        out_shape=out_shape,
        compiler_params=pltpu.CompilerParams(dimension_semantics=("parallel",),
                                             vmem_limit_bytes=VMEM_LIMIT),
        name="in_proj",
    )(x2d, ln_g, ln_b, w_bf16)
    return outs


def _mixer_kernel(z_ref, mu_ref, rwv_ref, wwa_hi_ref, wwa_lo_ref, gup_hi_ref, gup_lo_ref,
                  retv_ref, cos_ref, sin_ref, dmat_ref, qdec_ref, kdec_ref, bias_ref,
                  y_ref, s_ref, r_ref, carry_ref, kbuf_ref, vbuf_ref):
    n = pl.program_id(1)
    C = CHUNK

    @pl.when(n == 0)
    def _():
        s_ref[...] = jnp.zeros_like(s_ref)
        r_ref[...] = jnp.zeros_like(r_ref)
        carry_ref[...] = jnp.zeros_like(carry_ref)
        kbuf_ref[:, 0:BAND_PAD, :] = jnp.zeros((ATT_HEADS // 2, BAND_PAD, PAIR), BF16)
        vbuf_ref[:, 0:BAND_PAD, :] = jnp.zeros((ATT_HEADS // 2, BAND_PAD, PAIR), BF16)

    lane = lax.broadcasted_iota(jnp.int32, (C, PAIR), 1)
    head0 = lane < HEAD_DIM

    def blk(x):
        return jnp.concatenate([jnp.where(head0, x, 0.0), jnp.where(head0, 0.0, x)], axis=0)

    def unblk(xb):
        return xb[0:C] + xb[C:2 * C]

    rr = lax.broadcasted_iota(jnp.int32, (PAIR, PAIR), 0)
    cc = lax.broadcasted_iota(jnp.int32, (PAIR, PAIR), 1)
    tril_incl = rr >= cc
    tril_strict = rr > cc
    head_ones = ((rr >= HEAD_DIM) == (cc >= HEAD_DIM)).astype(BF16)

    def head_sum(x):
        hi, lo = _split2(x)
        return (jnp.dot(hi, head_ones, preferred_element_type=F32)
                + jnp.dot(lo, head_ones, preferred_element_type=F32))

    def head_norm(y, g, b, eps):
        mu = head_sum(y) * (1.0 / HEAD_DIM)
        d = y - mu
        var = head_sum(d * d) * (1.0 / HEAD_DIM)
        return d * lax.rsqrt(var + eps) * g + b

    zc = z_ref[:, 0:RWKV_COLS]
    row = lax.broadcasted_iota(jnp.int32, (C, RWKV_COLS), 0)
    prev = jnp.where(row == 0, carry_ref[7:8, :], pltpu.roll(zc, 1, 0))
    carry_ref[...] = zc[C - 8:C, :]
    zs = zc + (prev - zc) * mu_ref[...]
    r = zs[:, 0:RWKV_W]
    k = zs[:, RWKV_W:2 * RWKV_W]
    v = zs[:, 2 * RWKV_W:3 * RWKV_W]
    wa_l = zs[:, 3 * RWKV_W:3 * RWKV_W + PAIR]
    g_l = zs[:, 3 * RWKV_W + PAIR:RWKV_COLS]

    w0 = rwv_ref[0:1, :]
    a0 = rwv_ref[1:2, :]
    k_k = rwv_ref[2:3, :]
    k_a = rwv_ref[3:4, :]
    r_k = rwv_ref[4:5, :]
    ln_g = rwv_ref[5:6, :]
    ln_b = rwv_ref[6:7, :]

    wa_in = jnp.where(head0, jnp.tanh(wa_l), wa_l)
    lora = _dot_x3(wa_in, wwa_hi_ref[...], wwa_lo_ref[...])
    w_pre = w0 + lora[:, 0:RWKV_W]
    a = _sigmoid(a0 + lora[:, RWKV_W:2 * RWKV_W])
    g = _dot_x3(_sigmoid(g_l), gup_hi_ref[...], gup_lo_ref[...])
    sp = jnp.maximum(-w_pre, 0.0) + jnp.log(1.0 + jnp.exp(-jnp.abs(w_pre)))
    log_decay = -jnp.exp(-sp - 0.5)

    ld3 = jnp.concatenate(_split3(log_decay), axis=1)
    tri = (lax.broadcasted_iota(jnp.int32, (C, C), 0)
           >= lax.broadcasted_iota(jnp.int32, (C, C), 1)).astype(BF16)
    cs = jnp.dot(tri, ld3, preferred_element_type=F32)
    cum = cs[:, 0:RWKV_W] + cs[:, RWKV_W:2 * RWKV_W] + cs[:, 2 * RWKV_W:3 * RWKV_W]
    e_incl = jnp.exp(cum)
    e_excl = jnp.exp(cum - log_decay)
    e_inv = jnp.exp(-cum)

    kk = k * k_k
    k2 = k * (1.0 + (a - 1.0) * k_a)
    rkr = r * k2 * r_k

    y_rwkv = []
    for p in range(RWKV_HEADS // 2):
        sl = slice(p * PAIR, (p + 1) * PAIR)
        kk_p = kk[:, sl]
        ss = head_sum(kk_p * kk_p)
        kk_p = kk_p / jnp.maximum(jnp.sqrt(ss), 1e-12)
        a_hat = blk(-kk_p * e_excl[:, sl])
        r_hat = blk(r[:, sl] * e_incl[:, sl])
        b_til = blk(kk_p * a[:, sl] * e_inv[:, sl])
        k_til = blk(k2[:, sl] * e_inv[:, sl])
        v_b = blk(v[:, sl])
        s_old = s_ref[p]

        ar = jnp.concatenate([a_hat, r_hat], axis=0).astype(BF16)
        bk = jnp.concatenate([b_til, k_til], axis=0).astype(BF16)
        big = _dot_nt(ar, bk)
        a_ab = jnp.where(tril_strict, big[0:PAIR, 0:PAIR], 0.0)
        a_ak = jnp.where(tril_strict, big[0:PAIR, PAIR:2 * PAIR], 0.0)
        l_rb = jnp.where(tril_incl, big[PAIR:2 * PAIR, 0:PAIR], 0.0)
        l_rk = jnp.where(tril_incl, big[PAIR:2 * PAIR, PAIR:2 * PAIR], 0.0)
        ars = _dot_nt(ar, s_old)

        x = ars[0:PAIR] + _dot(a_ak, v_b)
        pw = a_ab
        for it in range(6):
            x = x + _dot(pw, x)
            if it < 5:
                pw = _dot(pw, pw)
        uv = jnp.concatenate([x, v_b], axis=0).astype(BF16)
        y_b = ars[PAIR:2 * PAIR] + _dot(jnp.concatenate([l_rb, l_rk], axis=1), uv)
        s_new = (s_old + _dot_tn(uv, bk)) * e_incl[C - 1:C, sl]
        s_ref[p] = s_new

        y_p = unblk(y_b)
        bonus = head_sum(rkr[:, sl]) * v[:, sl]
        y_p = head_norm(y_p, ln_g[:, sl], ln_b[:, sl], RWKV_GN_EPS) + bonus
        y_rwkv.append(y_p * g[:, sl])

    cos_t = cos_ref[...]
    sin_t = sin_ref[...]
    first_half = (lane % HEAD_DIM) < (HEAD_DIM // 2)

    def rope(t):
        partner = jnp.where(first_half, pltpu.roll(t, PAIR - HEAD_DIM // 2, 1),
                            pltpu.roll(t, HEAD_DIM // 2, 1))
        return t * cos_t + partner * sin_t

    gn_g = retv_ref[0:1, :]
    gn_b = retv_ref[1:2, :]
    chunk_decay = retv_ref[2:3, :]
    y_ret = []
    for p in range(RET_HEADS // 2):
        base = RET_OFF + p * PAIR
        sl = slice(p * PAIR, (p + 1) * PAIR)
        q = rope(z_ref[:, base:base + PAIR])
        kr = rope(z_ref[:, base + RET_W:base + RET_W + PAIR]) * (HEAD_DIM ** -0.5)
        vv = z_ref[:, base + 2 * RET_W:base + 2 * RET_W + PAIR]
        gg = z_ref[:, base + 3 * RET_W:base + 3 * RET_W + PAIR]
        q_b = blk(q)
        k_b = blk(kr)
        kd_b = blk(kr * kdec_ref[:, sl])
        v_b = blk(vv)
        r_old = r_ref[p]
        scores = _dot_nt(q_b, k_b) * dmat_ref[p]
        lhs = jnp.concatenate([scores, q_b * qdec_ref[p]], axis=1)
        rhs = jnp.concatenate([v_b, r_old], axis=0)
        y_p = unblk(_dot(lhs, rhs))
        r_ref[p] = r_old * chunk_decay[:, sl] + _dot_tn(kd_b, v_b)
        y_p = head_norm(y_p, gn_g[:, sl], gn_b[:, sl], RET_GN_EPS)
        y_ret.append(y_p * (gg * _sigmoid(gg)))

    off = pl.multiple_of(n * C, C)
    col = lax.broadcasted_iota(jnp.int32, (PAIR, BAND), 1)
    valid = col >= BAND_PAD - n * C
    y_att = []
    for p in range(ATT_HEADS // 2):
        base = ATT_OFF + p * PAIR
        q = z_ref[:, base:base + PAIR] * (HEAD_DIM ** -0.5)
        kbuf_ref[p, pl.ds(BAND_PAD + off, C), :] = z_ref[:, base + ATT_W:base + ATT_W + PAIR].astype(BF16)
        vbuf_ref[p, pl.ds(BAND_PAD + off, C), :] = z_ref[:, base + 2 * ATT_W:base + 2 * ATT_W + PAIR].astype(BF16)
        k_band = kbuf_ref[p, pl.ds(off, BAND), :]
        v_band = vbuf_ref[p, pl.ds(off, BAND), :]
        s = _dot_nt(blk(q), k_band) + bias_ref[p]
        s = jnp.where(valid, s, -1e30)
        m = jnp.max(s, axis=-1, keepdims=True)
        e = jnp.exp(s - m)
        l = jnp.sum(e, axis=-1, keepdims=True)
        o = _dot(e, v_band) / l
        y_att.append(jnp.where(head0, o[0:C], o[C:2 * C]))

    y_ref[...] = jnp.concatenate(y_rwkv + y_ret + y_att, axis=1)


def _mixers(z, mix_params, consts, batch, seq):
    nc = seq // CHUNK
    c2 = lambda b, n: (0, 0)
    c3 = lambda b, n: (0, 0, 0)
    (mu, rwv, wwa_hi, wwa_lo, gup_hi, gup_lo, retv, bias) = mix_params
    (cos_t, sin_t, dmat, qdec, kdec) = consts
    in_specs = [
        pl.BlockSpec((None, CHUNK, N_IN_COLS), lambda b, n: (b, n, 0)),
        pl.BlockSpec((1, RWKV_COLS), c2),
        pl.BlockSpec((8, RWKV_W), c2),
        pl.BlockSpec((PAIR, 2 * RWKV_W), c2),
        pl.BlockSpec((PAIR, 2 * RWKV_W), c2),
        pl.BlockSpec((GATE_LORA, RWKV_W), c2),
        pl.BlockSpec((GATE_LORA, RWKV_W), c2),
        pl.BlockSpec((8, RET_W), c2),
        pl.BlockSpec((CHUNK, PAIR), lambda b, n: (n, 0)),
        pl.BlockSpec((CHUNK, PAIR), lambda b, n: (n, 0)),
        pl.BlockSpec((RET_HEADS // 2, PAIR, PAIR), c3),
        pl.BlockSpec((RET_HEADS // 2, PAIR, PAIR), c3),
        pl.BlockSpec((CHUNK, RET_W), c2),
        pl.BlockSpec((ATT_HEADS // 2, PAIR, BAND), c3),
    ]
    return pl.pallas_call(
        _mixer_kernel,
        grid=(batch, nc),
        in_specs=in_specs,
        out_specs=pl.BlockSpec((None, CHUNK, D_MODEL), lambda b, n: (b, n, 0)),
        out_shape=jax.ShapeDtypeStruct((batch, seq, D_MODEL), F32),
        scratch_shapes=[
            pltpu.VMEM((RWKV_HEADS // 2, PAIR, PAIR), F32),
            pltpu.VMEM((RET_HEADS // 2, PAIR, PAIR), F32),
            pltpu.VMEM((8, RWKV_COLS), F32),
            pltpu.VMEM((ATT_HEADS // 2, BAND_PAD + seq, PAIR), BF16),
            pltpu.VMEM((ATT_HEADS // 2, BAND_PAD + seq, PAIR), BF16),
        ],
        compiler_params=pltpu.CompilerParams(dimension_semantics=("parallel", "arbitrary"),
                                             vmem_limit_bytes=VMEM_LIMIT),
        name="mixers",
    )(z, mu, rwv, wwa_hi, wwa_lo, gup_hi, gup_lo, retv, cos_t, sin_t, dmat, qdec, kdec, bias)


def _ffn_kernel(y_ref, x_ref, wout_ref, ln1g_ref, ln1b_ref, wup_ref, cw_ref, cb_ref, wdn_ref,
                ln2g_ref, ln2b_ref, o_ref, carry_ref, acc_ref):
    tm = x_ref.shape[0]

    @pl.when(pl.program_id(1) == 0)
    def _():
        carry_ref[...] = jnp.zeros_like(carry_ref)

    h = ALPHA * x_ref[...] + jnp.dot(y_ref[...].astype(BF16), wout_ref[...], preferred_element_type=F32)
    x1 = _layer_norm(h, ln1g_ref[...], ln1b_ref[...])
    x1b = x1.astype(BF16)
    row = lax.broadcasted_iota(jnp.int32, (tm, FFN_COLS), 0)

    def conv(u, cols):
        c6 = carry_ref[6:7, cols]
        c7 = carry_ref[7:8, cols]
        u1 = jnp.where(row == 0, c7, pltpu.roll(u, 1, 0))
        u2 = jnp.where(row == 0, c6, jnp.where(row == 1, c7, pltpu.roll(u, 2, 0)))
        carry_ref[:, cols] = u[tm - 8:tm, :]
        return (cb_ref[:, cols] + cw_ref[0:1, cols] * u2 + cw_ref[1:2, cols] * u1
                + cw_ref[2:3, cols] * u)

    for j in range(D_FF // FFN_COLS):
        gcols = slice(j * FFN_COLS, (j + 1) * FFN_COLS)
        vcols = slice(D_FF + j * FFN_COLS, D_FF + (j + 1) * FFN_COLS)
        gate = conv(jnp.dot(x1b, wup_ref[:, gcols], preferred_element_type=F32), gcols)
        val = conv(jnp.dot(x1b, wup_ref[:, vcols], preferred_element_type=F32), vcols)
        hmid = (gate * _sigmoid(gate) * val).astype(BF16)
        part = jnp.dot(hmid, wdn_ref[gcols, :], preferred_element_type=F32)
        if j == 0:
            acc_ref[...] = part
        else:
            acc_ref[...] += part
    o_ref[...] = _layer_norm(ALPHA * x1 + acc_ref[...], ln2g_ref[...], ln2b_ref[...])


def _out_ffn(y, x, wout, ln1g, ln1b, wup, cw, cb, wdn, ln2g, ln2b):
    batch, seq, _ = x.shape
    tm = min(ROW_TILE, seq)
    rows = lambda b, s: (b, s, 0)
    c2 = lambda b, s: (0, 0)
    one = pl.Buffered(1)
    return pl.pallas_call(
        _ffn_kernel,
        grid=(batch, seq // tm),
        in_specs=[
            pl.BlockSpec((None, tm, D_MODEL), rows),
            pl.BlockSpec((None, tm, D_MODEL), rows),
            pl.BlockSpec((D_MODEL, D_MODEL), c2, pipeline_mode=one),
            pl.BlockSpec((1, D_MODEL), c2),
            pl.BlockSpec((1, D_MODEL), c2),
            pl.BlockSpec((D_MODEL, 2 * D_FF), c2, pipeline_mode=one),
            pl.BlockSpec((CONV_W, 2 * D_FF), c2),
            pl.BlockSpec((1, 2 * D_FF), c2),
            pl.BlockSpec((D_FF, D_MODEL), c2, pipeline_mode=one),
            pl.BlockSpec((1, D_MODEL), c2),
            pl.BlockSpec((1, D_MODEL), c2),
        ],
        out_specs=pl.BlockSpec((None, tm, D_MODEL), rows),
        out_shape=jax.ShapeDtypeStruct((batch, seq, D_MODEL), F32),
        scratch_shapes=[pltpu.VMEM((8, 2 * D_FF), F32), pltpu.VMEM((tm, D_MODEL), F32)],
        compiler_params=pltpu.CompilerParams(dimension_semantics=("parallel", "arbitrary"),
                                             vmem_limit_bytes=VMEM_LIMIT),
        name="out_ffn",
    )(y, x, wout, ln1g, ln1b, wup, cw, cb, wdn, ln2g, ln2b)


def _hi_lo(w):
    hi = w.astype(BF16)
    return hi, (w - hi.astype(F32)).astype(BF16)


def _mixer_constants(seq):
    half = HEAD_DIM // 2
    lane = jnp.arange(PAIR)
    inv = ROPE_BASE ** (-jnp.arange(half, dtype=F32) / half)
    ang = jnp.arange(seq, dtype=F32)[:, None] * inv[None, :]
    idx = lane % half
    cos_t = jnp.cos(ang)[:, idx]
    sign = jnp.where((lane % HEAD_DIM) < half, -1.0, 1.0).astype(F32)
    sin_t = jnp.sin(ang)[:, idx] * sign[None, :]

    log_g = jnp.log(1.0 - jnp.exp2(-5.0 - jnp.arange(RET_HEADS, dtype=F32)))
    t = jnp.arange(CHUNK, dtype=F32)
    row_t = jnp.arange(PAIR) % CHUNK
    row_h = jnp.arange(PAIR) // CHUNK
    pairs = jnp.arange(RET_HEADS // 2)
    lg_rows = log_g[2 * pairs[:, None] + row_h[None, :]]
    same = (row_h[:, None] == row_h[None, :])
    dist = jnp.abs(row_t[:, None] - row_t[None, :]).astype(F32)
    dmat = jnp.where(same[None], jnp.exp(lg_rows[:, :, None] * dist[None]), 0.0)
    qdec = jnp.broadcast_to(jnp.exp(lg_rows * (row_t.astype(F32) + 1.0)[None, :])[:, :, None],
                            (RET_HEADS // 2, PAIR, PAIR))
    lg_lane = jnp.repeat(log_g, HEAD_DIM)
    kdec = jnp.exp(lg_lane[None, :] * (CHUNK - 1.0 - t)[:, None])
    chunk_decay = jnp.exp(lg_lane * CHUNK)
    return (cos_t, sin_t, dmat.astype(F32), qdec.astype(F32), kdec.astype(F32)), chunk_decay


def _attention_bias(rel_bias):
    i = jnp.arange(CHUNK)
    j = jnp.arange(BAND)
    rel = (i[:, None] + BAND_PAD) - j[None, :]
    rel_idx = jnp.clip(rel, -(CHUNK - 1), REL_CLIP) + (CHUNK - 1)
    bias = rel_bias.astype(F32)[:, rel_idx]
    return bias.reshape(ATT_HEADS // 2, PAIR, BAND)


def kernel(x, ln_in_g, ln_in_b, w_in, rw_mu, rw_w0, rw_w_up, rw_a0, rw_a_up, rw_g_up, rw_k_k, rw_k_a, rw_r_k, rw_ln_g, rw_ln_b, ret_gn_g, ret_gn_b, attn_rel_bias, w_out, ln1_g, ln1_b, ffn_w_up, ffn_conv_w, ffn_conv_b, ffn_w_down, ln2_g, ln2_b):
    batch, seq, _ = x.shape
    consts, chunk_decay = _mixer_constants(seq)
    row2 = lambda t: t.reshape(1, -1).astype(F32)
    xcur = x.reshape(batch * seq, D_MODEL)
    for l in range(DEPTH):
        if l == 0:
            z, xcur = _project(xcur, row2(ln_in_g), row2(ln_in_b), w_in[l].astype(BF16), True)
        else:
            (z,) = _project(xcur, row2(ln_in_g), row2(ln_in_b), w_in[l].astype(BF16), False)

        zero_row = jnp.zeros((RWKV_W,), F32)
        rwv = jnp.stack([rw_w0[l], rw_a0[l], rw_k_k[l], rw_k_a[l], rw_r_k[l].reshape(-1),
                         rw_ln_g[l], rw_ln_b[l], zero_row]).astype(F32)
        wwa = jnp.zeros((PAIR, 2 * RWKV_W), F32)
        wwa = wwa.at[0:DECAY_LORA, 0:RWKV_W].set(rw_w_up[l].astype(F32))
        wwa = wwa.at[DECAY_LORA:PAIR, RWKV_W:2 * RWKV_W].set(rw_a_up[l].astype(F32))
        wwa_hi, wwa_lo = _hi_lo(wwa)
        gup_hi, gup_lo = _hi_lo(rw_g_up[l].astype(F32))
        retv = jnp.concatenate([jnp.stack([ret_gn_g[l].astype(F32), ret_gn_b[l].astype(F32), chunk_decay]),
                                jnp.zeros((5, RET_W), F32)])
        mix_params = (row2(rw_mu[l]), rwv, wwa_hi, wwa_lo, gup_hi, gup_lo, retv,
                      _attention_bias(attn_rel_bias[l]))
        y = _mixers(z.reshape(batch, seq, N_IN_COLS), mix_params, consts, batch, seq)

        xcur = _out_ffn(y, xcur.reshape(batch, seq, D_MODEL), w_out[l].astype(BF16),
                        row2(ln1_g[l]), row2(ln1_b[l]), ffn_w_up[l].astype(BF16),
                        ffn_conv_w[l].astype(F32), row2(ffn_conv_b[l]), ffn_w_down[l].astype(BF16),
                        row2(ln2_g[l]), row2(ln2_b[l])).reshape(batch * seq, D_MODEL)
    return xcur.reshape(batch, seq, D_MODEL)
```

```python
import functools

import jax
import jax.numpy as jnp
from jax import lax
from jax.experimental import pallas as pl
from jax.experimental.pallas import tpu as pltpu

F32 = jnp.float32
BF16 = jnp.bfloat16

D_MODEL = 1024
DEPTH = 2
CHUNK = 64
HEAD_DIM = 64
PAIR = 2 * HEAD_DIM
RWKV_HEADS, RET_HEADS, ATT_HEADS = 6, 6, 4
RWKV_W, RET_W, ATT_W = 384, 384, 256
DECAY_LORA, AAA_LORA, GATE_LORA = 64, 64, 128
RWKV_COLS = 3 * RWKV_W + DECAY_LORA + AAA_LORA + GATE_LORA
RET_COLS = 4 * RET_W
ATT_COLS = 3 * ATT_W
N_IN_COLS = RWKV_COLS + RET_COLS + ATT_COLS
RET_OFF = RWKV_COLS
ATT_OFF = RWKV_COLS + RET_COLS
BAND_PREV_CHUNKS = 8
BAND_PAD = BAND_PREV_CHUNKS * CHUNK
BAND = BAND_PAD + CHUNK
REL_CLIP = 128
D_FF = 2816
CONV_W = 3
ALPHA = (2 * DEPTH) ** 0.25
ROPE_BASE = 10000.0
LN_EPS = 1e-5
RWKV_GN_EPS = 64e-5
RET_GN_EPS = 1e-5

ROW_TILE = 512
FFN_COLS = 256
MIX_BATCH = 2
VMEM_LIMIT = 56 * 1024 * 1024


def _dot(a, b):
    return jnp.dot(a.astype(BF16), b.astype(BF16), preferred_element_type=F32)


def _dot_nt(a, b):
    return lax.dot_general(a.astype(BF16), b.astype(BF16), (((1,), (1,)), ((), ())),
                           preferred_element_type=F32)


def _dot_tn(a, b):
    return lax.dot_general(a.astype(BF16), b.astype(BF16), (((0,), (0,)), ((), ())),
                           preferred_element_type=F32)


def _split2(x):
    hi = x.astype(BF16)
    lo = (x - hi.astype(F32)).astype(BF16)
    return hi, lo


def _split3(x):
    hi = x.astype(BF16)
    r1 = x - hi.astype(F32)
    mid = r1.astype(BF16)
    lo = (r1 - mid.astype(F32)).astype(BF16)
    return hi, mid, lo


def _dot_x3(x, w_hi, w_lo):
    x_hi, x_lo = _split2(x)
    return (jnp.dot(x_hi, w_hi, preferred_element_type=F32)
            + jnp.dot(x_hi, w_lo, preferred_element_type=F32)
            + jnp.dot(x_lo, w_hi, preferred_element_type=F32))


def _layer_norm(x, g, b, eps=LN_EPS):
    mu = jnp.mean(x, axis=-1, keepdims=True)
    d = x - mu
    var = jnp.mean(d * d, axis=-1, keepdims=True)
    return d * lax.rsqrt(var + eps) * g + b


def _sigmoid(x):
    return 1.0 / (1.0 + jnp.exp(-x))


def _proj_kernel(apply_ln, x_ref, g_ref, b_ref, w_ref, z_ref, *xn_ref):
    x = x_ref[...]
    if apply_ln:
        x = _layer_norm(x, g_ref[...], b_ref[...])
        xn_ref[0][...] = x
    z_ref[...] = jnp.dot(x.astype(BF16), w_ref[...], preferred_element_type=F32)


def _project(x2d, ln_g, ln_b, w_bf16, apply_ln):
    m = x2d.shape[0]
    tm = ROW_TILE
    row = lambda i: (i, 0)
    const = lambda i: (0, 0)
    out_shape = [jax.ShapeDtypeStruct((m, N_IN_COLS), F32)]
    out_specs = [pl.BlockSpec((tm, N_IN_COLS), row)]
    if apply_ln:
        out_shape.append(jax.ShapeDtypeStruct((m, D_MODEL), F32))
        out_specs.append(pl.BlockSpec((tm, D_MODEL), row))
    outs = pl.pallas_call(
        functools.partial(_proj_kernel, apply_ln),
        grid=(m // tm,),
        in_specs=[pl.BlockSpec((tm, D_MODEL), row),
                  pl.BlockSpec((1, D_MODEL), const),
                  pl.BlockSpec((1, D_MODEL), const),
                  pl.BlockSpec((D_MODEL, N_IN_COLS), const, pipeline_mode=pl.Buffered(1))],
        out_specs=out_specs,
        out_shape=out_shape,
        compiler_params=pltpu.CompilerParams(dimension_semantics=("parallel",),
                                             vmem_limit_bytes=VMEM_LIMIT),
        name="in_proj",
    )(x2d, ln_g, ln_b, w_bf16)
    return outs


def _mixer_kernel(z_ref, mu_ref, rwv_ref, wwa_hi_ref, wwa_lo_ref, gup_hi_ref, gup_lo_ref,
                  retv_ref, cos_ref, sin_ref, dmat_ref, qdec_ref, kdec_ref, bias_ref,
                  y_ref, s_ref, r_ref, carry_ref, kbuf_ref, vbuf_ref):
    n = pl.program_id(1)
    C = CHUNK

    @pl.when(n == 0)
    def _():
        s_ref[...] = jnp.zeros_like(s_ref)
        r_ref[...] = jnp.zeros_like(r_ref)
        carry_ref[...] = jnp.zeros_like(carry_ref)
        kbuf_ref[:, :, 0:BAND_PAD, :] = jnp.zeros((MIX_BATCH, ATT_HEADS // 2, BAND_PAD, PAIR), BF16)
        vbuf_ref[:, :, 0:BAND_PAD, :] = jnp.zeros((MIX_BATCH, ATT_HEADS // 2, BAND_PAD, PAIR), BF16)

    lane = lax.broadcasted_iota(jnp.int32, (C, PAIR), 1)
    head0 = lane < HEAD_DIM

    def blk(x):
        return jnp.concatenate([jnp.where(head0, x, 0.0), jnp.where(head0, 0.0, x)], axis=0)

    def unblk(xb):
        return xb[0:C] + xb[C:2 * C]

    rr = lax.broadcasted_iota(jnp.int32, (PAIR, PAIR), 0)
    cc = lax.broadcasted_iota(jnp.int32, (PAIR, PAIR), 1)
    tril_incl = rr >= cc
    tril_strict = rr > cc
    head_ones = ((rr >= HEAD_DIM) == (cc >= HEAD_DIM)).astype(BF16)

    def head_sums(slabs):
        x = jnp.concatenate(slabs, axis=0)
        m = x.shape[0]
        hi, lo = _split2(x)
        s = jnp.dot(jnp.concatenate([hi, lo], axis=0), head_ones, preferred_element_type=F32)
        s = s[0:m] + s[m:2 * m]
        return [s[i * C:(i + 1) * C] for i in range(len(slabs))]

    def head_norms(ys, gs, bs, epss):
        mus = head_sums(ys)
        ds = [y - mu * (1.0 / HEAD_DIM) for y, mu in zip(ys, mus)]
        vs = head_sums([d * d for d in ds])
        return [d * lax.rsqrt(var * (1.0 / HEAD_DIM) + eps) * g + b
                for d, var, g, b, eps in zip(ds, vs, gs, bs, epss)]

    w0 = rwv_ref[0:1, :]
    a0 = rwv_ref[1:2, :]
    k_k = rwv_ref[2:3, :]
    k_a = rwv_ref[3:4, :]
    r_k = rwv_ref[4:5, :]
    ln_g = rwv_ref[5:6, :]
    ln_b = rwv_ref[6:7, :]
    gn_g = retv_ref[0:1, :]
    gn_b = retv_ref[1:2, :]
    chunk_decay = retv_ref[2:3, :]
    cos_t = cos_ref[...]
    sin_t = sin_ref[...]
    first_half = (lane % HEAD_DIM) < (HEAD_DIM // 2)
    row = lax.broadcasted_iota(jnp.int32, (C, RWKV_COLS), 0)
    tri = (lax.broadcasted_iota(jnp.int32, (C, C), 0)
           >= lax.broadcasted_iota(jnp.int32, (C, C), 1)).astype(BF16)
    off = pl.multiple_of(n * C, C)
    col = lax.broadcasted_iota(jnp.int32, (PAIR, BAND), 1)
    valid = col >= BAND_PAD - n * C
    pairs = [slice(p * PAIR, (p + 1) * PAIR) for p in range(RWKV_HEADS // 2)]

    def rope(t):
        partner = jnp.where(first_half, pltpu.roll(t, PAIR - HEAD_DIM // 2, 1),
                            pltpu.roll(t, HEAD_DIM // 2, 1))
        return t * cos_t + partner * sin_t

    def one_sequence(i):
        zc = z_ref[i, :, 0:RWKV_COLS]
        prev = jnp.where(row == 0, carry_ref[i, 7:8, :], pltpu.roll(zc, 1, 0))
        carry_ref[i] = zc[C - 8:C, :]
        zs = zc + (prev - zc) * mu_ref[...]
        r = zs[:, 0:RWKV_W]
        k = zs[:, RWKV_W:2 * RWKV_W]
        v = zs[:, 2 * RWKV_W:3 * RWKV_W]
        wa_l = zs[:, 3 * RWKV_W:3 * RWKV_W + PAIR]
        g_l = zs[:, 3 * RWKV_W + PAIR:RWKV_COLS]

        wa_in = jnp.where(head0, jnp.tanh(wa_l), wa_l)
        lora = _dot_x3(wa_in, wwa_hi_ref[...], wwa_lo_ref[...])
        w_pre = w0 + lora[:, 0:RWKV_W]
        a = _sigmoid(a0 + lora[:, RWKV_W:2 * RWKV_W])
        g = _dot_x3(_sigmoid(g_l), gup_hi_ref[...], gup_lo_ref[...])
        sp = jnp.maximum(-w_pre, 0.0) + jnp.log(1.0 + jnp.exp(-jnp.abs(w_pre)))
        log_decay = -jnp.exp(-sp - 0.5)

        ld3 = jnp.concatenate(_split3(log_decay), axis=1)
        cs = jnp.dot(tri, ld3, preferred_element_type=F32)
        cum = cs[:, 0:RWKV_W] + cs[:, RWKV_W:2 * RWKV_W] + cs[:, 2 * RWKV_W:3 * RWKV_W]
        e_incl = jnp.exp(cum)
        e_excl = jnp.exp(cum - log_decay)
        e_inv = jnp.exp(-cum)

        kk = k * k_k
        k2 = k * (1.0 + (a - 1.0) * k_a)
        rkr = r * k2 * r_k
        sums = head_sums([kk[:, sl] * kk[:, sl] for sl in pairs] + [rkr[:, sl] for sl in pairs])
        kk_ss, rkr_sum = sums[0:3], sums[3:6]

        chains = []
        for p, sl in enumerate(pairs):
            kk_p = kk[:, sl] / jnp.maximum(jnp.sqrt(kk_ss[p]), 1e-12)
            a_hat = blk(-kk_p * e_excl[:, sl])
            r_hat = blk(r[:, sl] * e_incl[:, sl])
            b_til = blk(kk_p * a[:, sl] * e_inv[:, sl])
            k_til = blk(k2[:, sl] * e_inv[:, sl])
            chains.append(dict(
                i=i, p=p,
                ar=jnp.concatenate([a_hat, r_hat], axis=0).astype(BF16),
                bk=jnp.concatenate([b_til, k_til], axis=0).astype(BF16),
                v_b=blk(v[:, sl]).astype(BF16),
                p_end=e_incl[C - 1:C, sl]))
        return chains, dict(v=v, g=g, rkr_sum=rkr_sum)

    fillers = []
    y_ret = {}
    gates = {}
    y_att = {}

    def retention_tasks(i, p):
        st = {}
        sl = pairs[p]
        base = RET_OFF + p * PAIR

        def scores():
            q = rope(z_ref[i, :, base:base + PAIR])
            kr = rope(z_ref[i, :, base + RET_W:base + RET_W + PAIR]) * (HEAD_DIM ** -0.5)
            gg = z_ref[i, :, base + 3 * RET_W:base + 3 * RET_W + PAIR]
            gates[i, p] = gg * _sigmoid(gg)
            st["q_b"] = blk(q)
            st["kd_b"] = blk(kr * kdec_ref[:, sl])
            st["v_b"] = blk(z_ref[i, :, base + 2 * RET_W:base + 2 * RET_W + PAIR]).astype(BF16)
            st["scores"] = _dot_nt(st["q_b"], blk(kr)) * dmat_ref[p]

        def output():
            r_old = r_ref[i, p]
            lhs = jnp.concatenate([st["scores"], st["q_b"] * qdec_ref[p]], axis=1)
            rhs = jnp.concatenate([st["v_b"], r_old.astype(BF16)], axis=0)
            y_ret[i, p] = unblk(_dot(lhs, rhs))
            r_ref[i, p] = r_old * chunk_decay[:, sl] + _dot_tn(st["kd_b"], st["v_b"])

        return [scores, output]

    def attention_tasks(i, p):
        st = {}
        base = ATT_OFF + p * PAIR

        def scores():
            q = z_ref[i, :, base:base + PAIR] * (HEAD_DIM ** -0.5)
            kbuf_ref[i, p, pl.ds(BAND_PAD + off, C), :] = (
                z_ref[i, :, base + ATT_W:base + ATT_W + PAIR].astype(BF16))
            vbuf_ref[i, p, pl.ds(BAND_PAD + off, C), :] = (
                z_ref[i, :, base + 2 * ATT_W:base + 2 * ATT_W + PAIR].astype(BF16))
            k_band = kbuf_ref[i, p, pl.ds(off, BAND), :]
            s = _dot_nt(blk(q), k_band) + bias_ref[p]
            s = jnp.where(valid, s, -1e30)
            m = jnp.max(s, axis=-1, keepdims=True)
            st["e"] = jnp.exp(s - m)
            st["l"] = jnp.sum(st["e"], axis=-1, keepdims=True)

        def output():
            v_band = vbuf_ref[i, p, pl.ds(off, BAND), :]
            o = _dot(st["e"], v_band) / st["l"]
            y_att[i, p] = jnp.where(head0, o[0:C], o[C:2 * C])

        return [scores, output]

    seqs = range(MIX_BATCH)
    chains = []
    seq_vals = []
    for i in seqs:
        ch, vals = one_sequence(i)
        chains += ch
        seq_vals.append(vals)

    first, second = [], []
    for i in seqs:
        for p in range(RET_HEADS // 2):
            t = retention_tasks(i, p)
            first.append(t[0])
            second.append(t[1])
        for p in range(ATT_HEADS // 2):
            t = attention_tasks(i, p)
            first.append(t[0])
            second.append(t[1])
    fillers = first + second
    n_stages = 10
    per_stage = -(-len(fillers) // n_stages)

    def fill():
        for _ in range(per_stage):
            if fillers:
                fillers.pop(0)()

    for c in chains:
        c["s_old"] = s_ref[c["i"], c["p"]]
        big = _dot_nt(c["ar"], c["bk"])
        c["pw"] = jnp.where(tril_strict, big[0:PAIR, 0:PAIR], 0.0).astype(BF16)
        c["a_ak"] = jnp.where(tril_strict, big[0:PAIR, PAIR:2 * PAIR], 0.0)
        c["l_rbk"] = jnp.concatenate(
            [jnp.where(tril_incl, big[PAIR:2 * PAIR, 0:PAIR], 0.0),
             jnp.where(tril_incl, big[PAIR:2 * PAIR, PAIR:2 * PAIR], 0.0)], axis=1).astype(BF16)
    fill()
    for c in chains:
        c["ars"] = _dot_nt(c["ar"], c["s_old"])
    fill()
    for c in chains:
        c["x"] = c["ars"][0:PAIR] + _dot(c["a_ak"], c["v_b"])
    fill()
    for _ in range(5):
        for c in chains:
            res = jnp.dot(c["pw"], jnp.concatenate([c["x"].astype(BF16), c["pw"]], axis=1),
                          preferred_element_type=F32)
            c["x"] = c["x"] + res[:, 0:PAIR]
            c["pw"] = res[:, PAIR:2 * PAIR].astype(BF16)
        fill()
    for c in chains:
        c["x"] = c["x"] + jnp.dot(c["pw"], c["x"].astype(BF16), preferred_element_type=F32)
        c["uv"] = jnp.concatenate([c["x"].astype(BF16), c["v_b"]], axis=0)
    fill()
    y_rwkv = {}
    for c in chains:
        y_b = c["ars"][PAIR:2 * PAIR] + jnp.dot(c["l_rbk"], c["uv"], preferred_element_type=F32)
        y_rwkv[c["i"], c["p"]] = unblk(y_b)
    fill()
    for c in chains:
        s_ref[c["i"], c["p"]] = (c["s_old"] + _dot_tn(c["uv"], c["bk"])) * c["p_end"]
    while fillers:
        fillers.pop(0)()

    normed_all = head_norms(
        [y for i in seqs for y in ([y_rwkv[i, p] for p in range(3)] + [y_ret[i, p] for p in range(3)])],
        ([ln_g[:, sl] for sl in pairs] + [gn_g[:, sl] for sl in pairs]) * MIX_BATCH,
        ([ln_b[:, sl] for sl in pairs] + [gn_b[:, sl] for sl in pairs]) * MIX_BATCH,
        ([RWKV_GN_EPS] * 3 + [RET_GN_EPS] * 3) * MIX_BATCH)
    for i in seqs:
        vals = seq_vals[i]
        normed = normed_all[6 * i:6 * i + 6]
        out = []
        for p, sl in enumerate(pairs):
            out.append((normed[p] + vals["rkr_sum"][p] * vals["v"][:, sl]) * vals["g"][:, sl])
        for p in range(RET_HEADS // 2):
            out.append(normed[3 + p] * gates[i, p])
        for p in range(ATT_HEADS // 2):
            out.append(y_att[i, p])
        y_ref[i] = jnp.concatenate(out, axis=1)


def _mixers(z, mix_params, consts, batch, seq):
    nc = seq // CHUNK
    nb = MIX_BATCH
    c2 = lambda b, n: (0, 0)
    c3 = lambda b, n: (0, 0, 0)
    (mu, rwv, wwa_hi, wwa_lo, gup_hi, gup_lo, retv, bias) = mix_params
    (cos_t, sin_t, dmat, qdec, kdec) = consts
    in_specs = [
        pl.BlockSpec((nb, CHUNK, N_IN_COLS), lambda b, n: (b, n, 0)),
        pl.BlockSpec((1, RWKV_COLS), c2),
        pl.BlockSpec((8, RWKV_W), c2),
        pl.BlockSpec((PAIR, 2 * RWKV_W), c2),
        pl.BlockSpec((PAIR, 2 * RWKV_W), c2),
        pl.BlockSpec((GATE_LORA, RWKV_W), c2),
        pl.BlockSpec((GATE_LORA, RWKV_W), c2),
        pl.BlockSpec((8, RET_W), c2),
        pl.BlockSpec((CHUNK, PAIR), lambda b, n: (n, 0)),
        pl.BlockSpec((CHUNK, PAIR), lambda b, n: (n, 0)),
        pl.BlockSpec((RET_HEADS // 2, PAIR, PAIR), c3),
        pl.BlockSpec((RET_HEADS // 2, PAIR, PAIR), c3),
        pl.BlockSpec((CHUNK, RET_W), c2),
        pl.BlockSpec((ATT_HEADS // 2, PAIR, BAND), c3),
    ]
    return pl.pallas_call(
        _mixer_kernel,
        grid=(batch // nb, nc),
        in_specs=in_specs,
        out_specs=pl.BlockSpec((nb, CHUNK, D_MODEL), lambda b, n: (b, n, 0)),
        out_shape=jax.ShapeDtypeStruct((batch, seq, D_MODEL), F32),
        scratch_shapes=[
            pltpu.VMEM((nb, RWKV_HEADS // 2, PAIR, PAIR), F32),
            pltpu.VMEM((nb, RET_HEADS // 2, PAIR, PAIR), F32),
            pltpu.VMEM((nb, 8, RWKV_COLS), F32),
            pltpu.VMEM((nb, ATT_HEADS // 2, BAND_PAD + seq, PAIR), BF16),
            pltpu.VMEM((nb, ATT_HEADS // 2, BAND_PAD + seq, PAIR), BF16),
        ],
        compiler_params=pltpu.CompilerParams(dimension_semantics=("parallel", "arbitrary"),
                                             vmem_limit_bytes=VMEM_LIMIT),
        name="mixers",
    )(z, mu, rwv, wwa_hi, wwa_lo, gup_hi, gup_lo, retv, cos_t, sin_t, dmat, qdec, kdec, bias)


def _ffn_kernel(y_ref, x_ref, wout_ref, ln1g_ref, ln1b_ref, wup_ref, cw_ref, cb_ref, wdn_ref,
                ln2g_ref, ln2b_ref, o_ref, carry_ref, acc_ref):
    tm = x_ref.shape[0]

    @pl.when(pl.program_id(1) == 0)
    def _():
        carry_ref[...] = jnp.zeros_like(carry_ref)

    h = ALPHA * x_ref[...] + jnp.dot(y_ref[...].astype(BF16), wout_ref[...], preferred_element_type=F32)
    x1 = _layer_norm(h, ln1g_ref[...], ln1b_ref[...])
    x1b = x1.astype(BF16)
    row = lax.broadcasted_iota(jnp.int32, (tm, FFN_COLS), 0)

    def conv(u, cols):
        c6 = carry_ref[6:7, cols]
        c7 = carry_ref[7:8, cols]
        u1 = jnp.where(row == 0, c7, pltpu.roll(u, 1, 0))
        u2 = jnp.where(row == 0, c6, jnp.where(row == 1, c7, pltpu.roll(u, 2, 0)))
        carry_ref[:, cols] = u[tm - 8:tm, :]
        return (cb_ref[:, cols] + cw_ref[0:1, cols] * u2 + cw_ref[1:2, cols] * u1
                + cw_ref[2:3, cols] * u)

    for j in range(D_FF // FFN_COLS):
        gcols = slice(j * FFN_COLS, (j + 1) * FFN_COLS)
        vcols = slice(D_FF + j * FFN_COLS, D_FF + (j + 1) * FFN_COLS)
        gate = conv(jnp.dot(x1b, wup_ref[:, gcols], preferred_element_type=F32), gcols)
        val = conv(jnp.dot(x1b, wup_ref[:, vcols], preferred_element_type=F32), vcols)
        hmid = (gate * _sigmoid(gate) * val).astype(BF16)
        part = jnp.dot(hmid, wdn_ref[gcols, :], preferred_element_type=F32)
        if j == 0:
            acc_ref[...] = part
        else:
            acc_ref[...] += part
    o_ref[...] = _layer_norm(ALPHA * x1 + acc_ref[...], ln2g_ref[...], ln2b_ref[...])


def _out_ffn(y, x, wout, ln1g, ln1b, wup, cw, cb, wdn, ln2g, ln2b):
    batch, seq, _ = x.shape
    tm = min(ROW_TILE, seq)
    rows = lambda b, s: (b, s, 0)
    c2 = lambda b, s: (0, 0)
    one = pl.Buffered(1)
    return pl.pallas_call(
        _ffn_kernel,
        grid=(batch, seq // tm),
        in_specs=[
            pl.BlockSpec((None, tm, D_MODEL), rows),
            pl.BlockSpec((None, tm, D_MODEL), rows),
            pl.BlockSpec((D_MODEL, D_MODEL), c2, pipeline_mode=one),
            pl.BlockSpec((1, D_MODEL), c2),
            pl.BlockSpec((1, D_MODEL), c2),
            pl.BlockSpec((D_MODEL, 2 * D_FF), c2, pipeline_mode=one),
            pl.BlockSpec((CONV_W, 2 * D_FF), c2),
            pl.BlockSpec((1, 2 * D_FF), c2),
            pl.BlockSpec((D_FF, D_MODEL), c2, pipeline_mode=one),
            pl.BlockSpec((1, D_MODEL), c2),
            pl.BlockSpec((1, D_MODEL), c2),
        ],
        out_specs=pl.BlockSpec((None, tm, D_MODEL), rows),
        out_shape=jax.ShapeDtypeStruct((batch, seq, D_MODEL), F32),
        scratch_shapes=[pltpu.VMEM((8, 2 * D_FF), F32), pltpu.VMEM((tm, D_MODEL), F32)],
        compiler_params=pltpu.CompilerParams(dimension_semantics=("parallel", "arbitrary"),
                                             vmem_limit_bytes=VMEM_LIMIT),
        name="out_ffn",
    )(y, x, wout, ln1g, ln1b, wup, cw, cb, wdn, ln2g, ln2b)


def _hi_lo(w):
    hi = w.astype(BF16)
    return hi, (w - hi.astype(F32)).astype(BF16)


def _mixer_constants(seq):
    half = HEAD_DIM // 2
    lane = jnp.arange(PAIR)
    inv = ROPE_BASE ** (-jnp.arange(half, dtype=F32) / half)
    ang = jnp.arange(seq, dtype=F32)[:, None] * inv[None, :]
    idx = lane % half
    cos_t = jnp.cos(ang)[:, idx]
    sign = jnp.where((lane % HEAD_DIM) < half, -1.0, 1.0).astype(F32)
    sin_t = jnp.sin(ang)[:, idx] * sign[None, :]

    log_g = jnp.log(1.0 - jnp.exp2(-5.0 - jnp.arange(RET_HEADS, dtype=F32)))
    t = jnp.arange(CHUNK, dtype=F32)
    row_t = jnp.arange(PAIR) % CHUNK
    row_h = jnp.arange(PAIR) // CHUNK
    pairs = jnp.arange(RET_HEADS // 2)
    lg_rows = log_g[2 * pairs[:, None] + row_h[None, :]]
    same = (row_h[:, None] == row_h[None, :])
    dist = jnp.abs(row_t[:, None] - row_t[None, :]).astype(F32)
    dmat = jnp.where(same[None], jnp.exp(lg_rows[:, :, None] * dist[None]), 0.0)
    qdec = jnp.broadcast_to(jnp.exp(lg_rows * (row_t.astype(F32) + 1.0)[None, :])[:, :, None],
                            (RET_HEADS // 2, PAIR, PAIR))
    lg_lane = jnp.repeat(log_g, HEAD_DIM)
    kdec = jnp.exp(lg_lane[None, :] * (CHUNK - 1.0 - t)[:, None])
    chunk_decay = jnp.exp(lg_lane * CHUNK)
    return (cos_t, sin_t, dmat.astype(F32), qdec.astype(F32), kdec.astype(F32)), chunk_decay


def _attention_bias(rel_bias):
    i = jnp.arange(CHUNK)
    j = jnp.arange(BAND)
    rel = (i[:, None] + BAND_PAD) - j[None, :]
    rel_idx = jnp.clip(rel, -(CHUNK - 1), REL_CLIP) + (CHUNK - 1)
    bias = rel_bias.astype(F32)[:, rel_idx]
    return bias.reshape(ATT_HEADS // 2, PAIR, BAND)


def kernel(x, ln_in_g, ln_in_b, w_in, rw_mu, rw_w0, rw_w_up, rw_a0, rw_a_up, rw_g_up, rw_k_k, rw_k_a, rw_r_k, rw_ln_g, rw_ln_b, ret_gn_g, ret_gn_b, attn_rel_bias, w_out, ln1_g, ln1_b, ffn_w_up, ffn_conv_w, ffn_conv_b, ffn_w_down, ln2_g, ln2_b):
    batch, seq, _ = x.shape
    consts, chunk_decay = _mixer_constants(seq)
    row2 = lambda t: t.reshape(1, -1).astype(F32)
    xcur = x.reshape(batch * seq, D_MODEL)
    for l in range(DEPTH):
        if l == 0:
            z, xcur = _project(xcur, row2(ln_in_g), row2(ln_in_b), w_in[l].astype(BF16), True)
        else:
            (z,) = _project(xcur, row2(ln_in_g), row2(ln_in_b), w_in[l].astype(BF16), False)

        zero_row = jnp.zeros((RWKV_W,), F32)
        rwv = jnp.stack([rw_w0[l], rw_a0[l], rw_k_k[l], rw_k_a[l], rw_r_k[l].reshape(-1),
                         rw_ln_g[l], rw_ln_b[l], zero_row]).astype(F32)
        wwa = jnp.zeros((PAIR, 2 * RWKV_W), F32)
        wwa = wwa.at[0:DECAY_LORA, 0:RWKV_W].set(rw_w_up[l].astype(F32))
        wwa = wwa.at[DECAY_LORA:PAIR, RWKV_W:2 * RWKV_W].set(rw_a_up[l].astype(F32))
        wwa_hi, wwa_lo = _hi_lo(wwa)
        gup_hi, gup_lo = _hi_lo(rw_g_up[l].astype(F32))
        retv = jnp.concatenate([jnp.stack([ret_gn_g[l].astype(F32), ret_gn_b[l].astype(F32), chunk_decay]),
                                jnp.zeros((5, RET_W), F32)])
        mix_params = (row2(rw_mu[l]), rwv, wwa_hi, wwa_lo, gup_hi, gup_lo, retv,
                      _attention_bias(attn_rel_bias[l]))
        y = _mixers(z.reshape(batch, seq, N_IN_COLS), mix_params, consts, batch, seq)

        xcur = _out_ffn(y, xcur.reshape(batch, seq, D_MODEL), w_out[l].astype(BF16),
                        row2(ln1_g[l]), row2(ln1_b[l]), ffn_w_up[l].astype(BF16),
                        ffn_conv_w[l].astype(F32), row2(ffn_conv_b[l]), ffn_w_down[l].astype(BF16),
                        row2(ln2_g[l]), row2(ln2_b[l])).reshape(batch * seq, D_MODEL)
    return xcur.reshape(batch, seq, D_MODEL)
```

```python
import functools

import jax
import jax.numpy as jnp
from jax import lax
from jax.experimental import pallas as pl
from jax.experimental.pallas import tpu as pltpu

F32 = jnp.float32
BF16 = jnp.bfloat16

D_MODEL = 1024
DEPTH = 2
CHUNK = 64
HEAD_DIM = 64
PAIR = 2 * HEAD_DIM
RWKV_HEADS, RET_HEADS, ATT_HEADS = 6, 6, 4
RWKV_W, RET_W, ATT_W = 384, 384, 256
DECAY_LORA, AAA_LORA, GATE_LORA = 64, 64, 128
RWKV_COLS = 3 * RWKV_W + DECAY_LORA + AAA_LORA + GATE_LORA
RET_COLS = 4 * RET_W
ATT_COLS = 3 * ATT_W
N_IN_COLS = RWKV_COLS + RET_COLS + ATT_COLS
RET_OFF = RWKV_COLS
ATT_OFF = RWKV_COLS + RET_COLS
BAND_PREV_CHUNKS = 8
BAND_PAD = BAND_PREV_CHUNKS * CHUNK
BAND = BAND_PAD + CHUNK
REL_CLIP = 128
D_FF = 2816
CONV_W = 3
ALPHA = (2 * DEPTH) ** 0.25
ROPE_BASE = 10000.0
LN_EPS = 1e-5
RWKV_GN_EPS = 64e-5
RET_GN_EPS = 1e-5

ROW_TILE = 512
FFN_COLS = 256
FFN_LOOKAHEAD = 2
FFN_SLOTS = FFN_LOOKAHEAD + 1
MIX_BATCH = 2
VMEM_LIMIT = 56 * 1024 * 1024


def _dot(a, b):
    return jnp.dot(a.astype(BF16), b.astype(BF16), preferred_element_type=F32)


def _dot_nt(a, b):
    return lax.dot_general(a.astype(BF16), b.astype(BF16), (((1,), (1,)), ((), ())),
                           preferred_element_type=F32)


def _dot_tn(a, b):
    return lax.dot_general(a.astype(BF16), b.astype(BF16), (((0,), (0,)), ((), ())),
                           preferred_element_type=F32)


def _split2(x):
    hi = x.astype(BF16)
    lo = (x - hi.astype(F32)).astype(BF16)
    return hi, lo


def _split3(x):
    hi = x.astype(BF16)
    r1 = x - hi.astype(F32)
    mid = r1.astype(BF16)
    lo = (r1 - mid.astype(F32)).astype(BF16)
    return hi, mid, lo


def _dot_x3(x, w_hi, w_lo):
    x_hi, x_lo = _split2(x)
    return (jnp.dot(x_hi, w_hi, preferred_element_type=F32)
            + jnp.dot(x_hi, w_lo, preferred_element_type=F32)
            + jnp.dot(x_lo, w_hi, preferred_element_type=F32))


def _layer_norm(x, g, b, eps=LN_EPS):
    mu = jnp.mean(x, axis=-1, keepdims=True)
    d = x - mu
    var = jnp.mean(d * d, axis=-1, keepdims=True)
    return d * lax.rsqrt(var + eps) * g + b


def _sigmoid(x):
    return 1.0 / (1.0 + jnp.exp(-x))


def _proj_kernel(apply_ln, x_ref, g_ref, b_ref, w_ref, z_ref, *xn_ref):
    x = x_ref[...]
    if apply_ln:
        x = _layer_norm(x, g_ref[...], b_ref[...])
        xn_ref[0][...] = x
    z_ref[...] = jnp.dot(x.astype(BF16), w_ref[...], preferred_element_type=F32)


def _project(x2d, ln_g, ln_b, w_bf16, apply_ln):
    m = x2d.shape[0]
    tm = ROW_TILE
    row = lambda i: (i, 0)
    const = lambda i: (0, 0)
    out_shape = [jax.ShapeDtypeStruct((m, N_IN_COLS), F32)]
    out_specs = [pl.BlockSpec((tm, N_IN_COLS), row)]
    if apply_ln:
        out_shape.append(jax.ShapeDtypeStruct((m, D_MODEL), F32))
        out_specs.append(pl.BlockSpec((tm, D_MODEL), row))
    outs = pl.pallas_call(
        functools.partial(_proj_kernel, apply_ln),
        grid=(m // tm,),
        in_specs=[pl.BlockSpec((tm, D_MODEL), row),
                  pl.BlockSpec((1, D_MODEL), const),
                  pl.BlockSpec((1, D_MODEL), const),
                  pl.BlockSpec((D_MODEL, N_IN_COLS), const, pipeline_mode=pl.Buffered(1))],
        out_specs=out_specs,
        out_shape=out_shape,
        compiler_params=pltpu.CompilerParams(dimension_semantics=("parallel",),
                                             vmem_limit_bytes=VMEM_LIMIT),
        name="in_proj",
    )(x2d, ln_g, ln_b, w_bf16)
    return outs


def _mixer_kernel(z_ref, mu_ref, rwv_ref, wwa_hi_ref, wwa_lo_ref, gup_hi_ref, gup_lo_ref,
                  retv_ref, cos_ref, sin_ref, dmat_ref, qdec_ref, kdec_ref, bias_ref,
                  y_ref, s_ref, r_ref, carry_ref, kbuf_ref, vbuf_ref):
    n = pl.program_id(1)
    C = CHUNK

    @pl.when(n == 0)
    def _():
        s_ref[...] = jnp.zeros_like(s_ref)
        r_ref[...] = jnp.zeros_like(r_ref)
        carry_ref[...] = jnp.zeros_like(carry_ref)
        kbuf_ref[:, :, 0:BAND_PAD, :] = jnp.zeros((MIX_BATCH, ATT_HEADS // 2, BAND_PAD, PAIR), BF16)
        vbuf_ref[:, :, 0:BAND_PAD, :] = jnp.zeros((MIX_BATCH, ATT_HEADS // 2, BAND_PAD, PAIR), BF16)

    lane = lax.broadcasted_iota(jnp.int32, (C, PAIR), 1)
    head0 = lane < HEAD_DIM

    def blk(x):
        return jnp.concatenate([jnp.where(head0, x, 0.0), jnp.where(head0, 0.0, x)], axis=0)

    def unblk(xb):
        return xb[0:C] + xb[C:2 * C]

    rr = lax.broadcasted_iota(jnp.int32, (PAIR, PAIR), 0)
    cc = lax.broadcasted_iota(jnp.int32, (PAIR, PAIR), 1)
    tril_incl = rr >= cc
    tril_strict = rr > cc
    head_ones = ((rr >= HEAD_DIM) == (cc >= HEAD_DIM)).astype(BF16)

    def head_sums(slabs):
        x = jnp.concatenate(slabs, axis=0)
        m = x.shape[0]
        hi, lo = _split2(x)
        s = jnp.dot(jnp.concatenate([hi, lo], axis=0), head_ones, preferred_element_type=F32)
        s = s[0:m] + s[m:2 * m]
        return [s[i * C:(i + 1) * C] for i in range(len(slabs))]

    def head_norms(ys, gs, bs, epss):
        mus = head_sums(ys)
        ds = [y - mu * (1.0 / HEAD_DIM) for y, mu in zip(ys, mus)]
        vs = head_sums([d * d for d in ds])
        return [d * lax.rsqrt(var * (1.0 / HEAD_DIM) + eps) * g + b
                for d, var, g, b, eps in zip(ds, vs, gs, bs, epss)]

    w0 = rwv_ref[0:1, :]
    a0 = rwv_ref[1:2, :]
    k_k = rwv_ref[2:3, :]
    k_a = rwv_ref[3:4, :]
    r_k = rwv_ref[4:5, :]
    ln_g = rwv_ref[5:6, :]
    ln_b = rwv_ref[6:7, :]
    gn_g = retv_ref[0:1, :]
    gn_b = retv_ref[1:2, :]
    chunk_decay = retv_ref[2:3, :]
    cos_t = cos_ref[...]
    sin_t = sin_ref[...]
    first_half = (lane % HEAD_DIM) < (HEAD_DIM // 2)
    row = lax.broadcasted_iota(jnp.int32, (C, RWKV_COLS), 0)
    tri = (lax.broadcasted_iota(jnp.int32, (C, C), 0)
           >= lax.broadcasted_iota(jnp.int32, (C, C), 1)).astype(BF16)
    off = pl.multiple_of(n * C, C)
    col = lax.broadcasted_iota(jnp.int32, (PAIR, BAND), 1)
    valid = col >= BAND_PAD - n * C
    pairs = [slice(p * PAIR, (p + 1) * PAIR) for p in range(RWKV_HEADS // 2)]

    def rope(t):
        partner = jnp.where(first_half, pltpu.roll(t, PAIR - HEAD_DIM // 2, 1),
                            pltpu.roll(t, HEAD_DIM // 2, 1))
        return t * cos_t + partner * sin_t

    def one_sequence(i):
        zc = z_ref[i, :, 0:RWKV_COLS]
        prev = jnp.where(row == 0, carry_ref[i, 7:8, :], pltpu.roll(zc, 1, 0))
        carry_ref[i] = zc[C - 8:C, :]
        zs = zc + (prev - zc) * mu_ref[...]
        r = zs[:, 0:RWKV_W]
        k = zs[:, RWKV_W:2 * RWKV_W]
        v = zs[:, 2 * RWKV_W:3 * RWKV_W]
        wa_l = zs[:, 3 * RWKV_W:3 * RWKV_W + PAIR]
        g_l = zs[:, 3 * RWKV_W + PAIR:RWKV_COLS]

        wa_in = jnp.where(head0, jnp.tanh(wa_l), wa_l)
        lora = _dot_x3(wa_in, wwa_hi_ref[...], wwa_lo_ref[...])
        w_pre = w0 + lora[:, 0:RWKV_W]
        a = _sigmoid(a0 + lora[:, RWKV_W:2 * RWKV_W])
        g = _dot_x3(_sigmoid(g_l), gup_hi_ref[...], gup_lo_ref[...])
        sp = jnp.maximum(-w_pre, 0.0) + jnp.log(1.0 + jnp.exp(-jnp.abs(w_pre)))
        log_decay = -jnp.exp(-sp - 0.5)

        ld3 = jnp.concatenate(_split3(log_decay), axis=1)
        cs = jnp.dot(tri, ld3, preferred_element_type=F32)
        cum = cs[:, 0:RWKV_W] + cs[:, RWKV_W:2 * RWKV_W] + cs[:, 2 * RWKV_W:3 * RWKV_W]
        e_incl = jnp.exp(cum)
        e_excl = jnp.exp(cum - log_decay)
        e_inv = jnp.exp(-cum)

        kk = k * k_k
        k2 = k * (1.0 + (a - 1.0) * k_a)
        rkr = r * k2 * r_k
        sums = head_sums([kk[:, sl] * kk[:, sl] for sl in pairs] + [rkr[:, sl] for sl in pairs])
        kk_ss, rkr_sum = sums[0:3], sums[3:6]

        chains = []
        for p, sl in enumerate(pairs):
            kk_p = kk[:, sl] / jnp.maximum(jnp.sqrt(kk_ss[p]), 1e-12)
            a_hat = blk(-kk_p * e_excl[:, sl])
            r_hat = blk(r[:, sl] * e_incl[:, sl])
            b_til = blk(kk_p * a[:, sl] * e_inv[:, sl])
            k_til = blk(k2[:, sl] * e_inv[:, sl])
            chains.append(dict(
                i=i, p=p,
                ar=jnp.concatenate([a_hat, r_hat], axis=0).astype(BF16),
                bk=jnp.concatenate([b_til, k_til], axis=0).astype(BF16),
                v_b=blk(v[:, sl]).astype(BF16),
                p_end=e_incl[C - 1:C, sl]))
        return chains, dict(v=v, g=g, rkr_sum=rkr_sum)

    fillers = []
    y_ret = {}
    gates = {}
    y_att = {}

    def retention_tasks(i, p):
        st = {}
        sl = pairs[p]
        base = RET_OFF + p * PAIR

        def scores():
            q = rope(z_ref[i, :, base:base + PAIR])
            kr = rope(z_ref[i, :, base + RET_W:base + RET_W + PAIR]) * (HEAD_DIM ** -0.5)
            gg = z_ref[i, :, base + 3 * RET_W:base + 3 * RET_W + PAIR]
            gates[i, p] = gg * _sigmoid(gg)
            st["q_b"] = blk(q)
            st["kd_b"] = blk(kr * kdec_ref[:, sl])
            st["v_b"] = blk(z_ref[i, :, base + 2 * RET_W:base + 2 * RET_W + PAIR]).astype(BF16)
            st["scores"] = _dot_nt(st["q_b"], blk(kr)) * dmat_ref[p]

        def output():
            r_old = r_ref[i, p]
            lhs = jnp.concatenate([st["scores"], st["q_b"] * qdec_ref[p]], axis=1)
            rhs = jnp.concatenate([st["v_b"], r_old.astype(BF16)], axis=0)
            y_ret[i, p] = unblk(_dot(lhs, rhs))
            r_ref[i, p] = r_old * chunk_decay[:, sl] + _dot_tn(st["kd_b"], st["v_b"])

        return [scores, output]

    def attention_tasks(i, p):
        st = {}
        base = ATT_OFF + p * PAIR

        def scores():
            q = z_ref[i, :, base:base + PAIR] * (HEAD_DIM ** -0.5)
            kbuf_ref[i, p, pl.ds(BAND_PAD + off, C), :] = (
                z_ref[i, :, base + ATT_W:base + ATT_W + PAIR].astype(BF16))
            vbuf_ref[i, p, pl.ds(BAND_PAD + off, C), :] = (
                z_ref[i, :, base + 2 * ATT_W:base + 2 * ATT_W + PAIR].astype(BF16))
            k_band = kbuf_ref[i, p, pl.ds(off, BAND), :]
            s = _dot_nt(blk(q), k_band) + bias_ref[p]
            s = jnp.where(valid, s, -1e30)
            m = jnp.max(s, axis=-1, keepdims=True)
            st["e"] = jnp.exp(s - m)
            st["l"] = jnp.sum(st["e"], axis=-1, keepdims=True)

        def output():
            v_band = vbuf_ref[i, p, pl.ds(off, BAND), :]
            o = _dot(st["e"], v_band) / st["l"]
            y_att[i, p] = jnp.where(head0, o[0:C], o[C:2 * C])

        return [scores, output]

    seqs = range(MIX_BATCH)
    chains = []
    seq_vals = []
    for i in seqs:
        ch, vals = one_sequence(i)
        chains += ch
        seq_vals.append(vals)

    first, second = [], []
    for i in seqs:
        for p in range(RET_HEADS // 2):
            t = retention_tasks(i, p)
            first.append(t[0])
            second.append(t[1])
        for p in range(ATT_HEADS // 2):
            t = attention_tasks(i, p)
            first.append(t[0])
            second.append(t[1])
    fillers = first + second
    n_stages = 10
    per_stage = -(-len(fillers) // n_stages)

    def fill():
        for _ in range(per_stage):
            if fillers:
                fillers.pop(0)()

    for c in chains:
        c["s_old"] = s_ref[c["i"], c["p"]]
        big = _dot_nt(c["ar"], c["bk"])
        c["pw"] = jnp.where(tril_strict, big[0:PAIR, 0:PAIR], 0.0).astype(BF16)
        c["a_ak"] = jnp.where(tril_strict, big[0:PAIR, PAIR:2 * PAIR], 0.0)
        c["l_rbk"] = jnp.concatenate(
            [jnp.where(tril_incl, big[PAIR:2 * PAIR, 0:PAIR], 0.0),
             jnp.where(tril_incl, big[PAIR:2 * PAIR, PAIR:2 * PAIR], 0.0)], axis=1).astype(BF16)
    fill()
    for c in chains:
        c["ars"] = _dot_nt(c["ar"], c["s_old"])
    fill()
    for c in chains:
        c["x"] = c["ars"][0:PAIR] + _dot(c["a_ak"], c["v_b"])
    fill()
    for _ in range(5):
        for c in chains:
            res = jnp.dot(c["pw"], jnp.concatenate([c["x"].astype(BF16), c["pw"]], axis=1),
                          preferred_element_type=F32)
            c["x"] = c["x"] + res[:, 0:PAIR]
            c["pw"] = res[:, PAIR:2 * PAIR].astype(BF16)
        fill()
    for c in chains:
        c["x"] = c["x"] + jnp.dot(c["pw"], c["x"].astype(BF16), preferred_element_type=F32)
        c["uv"] = jnp.concatenate([c["x"].astype(BF16), c["v_b"]], axis=0)
    fill()
    y_rwkv = {}
    for c in chains:
        y_b = c["ars"][PAIR:2 * PAIR] + jnp.dot(c["l_rbk"], c["uv"], preferred_element_type=F32)
        y_rwkv[c["i"], c["p"]] = unblk(y_b)
    fill()
    for c in chains:
        s_ref[c["i"], c["p"]] = (c["s_old"] + _dot_tn(c["uv"], c["bk"])) * c["p_end"]
    while fillers:
        fillers.pop(0)()

    normed_all = head_norms(
        [y for i in seqs for y in ([y_rwkv[i, p] for p in range(3)] + [y_ret[i, p] for p in range(3)])],
        ([ln_g[:, sl] for sl in pairs] + [gn_g[:, sl] for sl in pairs]) * MIX_BATCH,
        ([ln_b[:, sl] for sl in pairs] + [gn_b[:, sl] for sl in pairs]) * MIX_BATCH,
        ([RWKV_GN_EPS] * 3 + [RET_GN_EPS] * 3) * MIX_BATCH)
    for i in seqs:
        vals = seq_vals[i]
        normed = normed_all[6 * i:6 * i + 6]
        out = []
        for p, sl in enumerate(pairs):
            out.append((normed[p] + vals["rkr_sum"][p] * vals["v"][:, sl]) * vals["g"][:, sl])
        for p in range(RET_HEADS // 2):
            out.append(normed[3 + p] * gates[i, p])
        for p in range(ATT_HEADS // 2):
            out.append(y_att[i, p])
        y_ref[i] = jnp.concatenate(out, axis=1)


def _mixers(z, mix_params, consts, batch, seq):
    nc = seq // CHUNK
    nb = MIX_BATCH
    c2 = lambda b, n: (0, 0)
    c3 = lambda b, n: (0, 0, 0)
    (mu, rwv, wwa_hi, wwa_lo, gup_hi, gup_lo, retv, bias) = mix_params
    (cos_t, sin_t, dmat, qdec, kdec) = consts
    in_specs = [
        pl.BlockSpec((nb, CHUNK, N_IN_COLS), lambda b, n: (b, n, 0)),
        pl.BlockSpec((1, RWKV_COLS), c2),
        pl.BlockSpec((8, RWKV_W), c2),
        pl.BlockSpec((PAIR, 2 * RWKV_W), c2),
        pl.BlockSpec((PAIR, 2 * RWKV_W), c2),
        pl.BlockSpec((GATE_LORA, RWKV_W), c2),
        pl.BlockSpec((GATE_LORA, RWKV_W), c2),
        pl.BlockSpec((8, RET_W), c2),
        pl.BlockSpec((CHUNK, PAIR), lambda b, n: (n, 0)),
        pl.BlockSpec((CHUNK, PAIR), lambda b, n: (n, 0)),
        pl.BlockSpec((RET_HEADS // 2, PAIR, PAIR), c3),
        pl.BlockSpec((RET_HEADS // 2, PAIR, PAIR), c3),
        pl.BlockSpec((CHUNK, RET_W), c2),
        pl.BlockSpec((ATT_HEADS // 2, PAIR, BAND), c3),
    ]
    return pl.pallas_call(
        _mixer_kernel,
        grid=(batch // nb, nc),
        in_specs=in_specs,
        out_specs=pl.BlockSpec((nb, CHUNK, D_MODEL), lambda b, n: (b, n, 0)),
        out_shape=jax.ShapeDtypeStruct((batch, seq, D_MODEL), F32),
        scratch_shapes=[
            pltpu.VMEM((nb, RWKV_HEADS // 2, PAIR, PAIR), F32),
            pltpu.VMEM((nb, RET_HEADS // 2, PAIR, PAIR), F32),
            pltpu.VMEM((nb, 8, RWKV_COLS), F32),
            pltpu.VMEM((nb, ATT_HEADS // 2, BAND_PAD + seq, PAIR), BF16),
            pltpu.VMEM((nb, ATT_HEADS // 2, BAND_PAD + seq, PAIR), BF16),
        ],
        compiler_params=pltpu.CompilerParams(dimension_semantics=("parallel", "arbitrary"),
                                             vmem_limit_bytes=VMEM_LIMIT),
        name="mixers",
    )(z, mu, rwv, wwa_hi, wwa_lo, gup_hi, gup_lo, retv, cos_t, sin_t, dmat, qdec, kdec, bias)


def _ffn_kernel(y_ref, x_ref, wout_ref, ln1g_ref, ln1b_ref, wup_ref, cw_ref, cb_ref, wdn_ref,
                ln2g_ref, ln2b_ref, o_ref, carry_ref, acc_ref, x1_ref, *slot_refs):
    tm = x_ref.shape[0]

    @pl.when(pl.program_id(1) == 0)
    def _():
        carry_ref[...] = jnp.zeros_like(carry_ref)

    h = ALPHA * x_ref[...] + jnp.dot(y_ref[...].astype(BF16), wout_ref[...], preferred_element_type=F32)
    x1 = _layer_norm(h, ln1g_ref[...], ln1b_ref[...])
    x1_ref[...] = x1.astype(BF16)
    bw = 2 * FFN_COLS

    u_refs = slot_refs[0:FFN_SLOTS]
    h_refs = slot_refs[FFN_SLOTS:2 * FFN_SLOTS]

    def up(j):
        u_ref = u_refs[j % FFN_SLOTS]
        cols = slice(j * bw, (j + 1) * bw)
        u_ref[0:8, :] = carry_ref[:, cols]
        u_ref[8:8 + tm, :] = jnp.dot(x1_ref[...], wup_ref[:, cols], preferred_element_type=F32)
        carry_ref[:, cols] = u_ref[tm:tm + 8, :]

    def gated(j):
        u_ref = u_refs[j % FFN_SLOTS]
        cols = slice(j * bw, (j + 1) * bw)
        c = (cb_ref[:, cols] + cw_ref[0:1, cols] * u_ref[6:6 + tm, :]
             + cw_ref[1:2, cols] * u_ref[7:7 + tm, :] + cw_ref[2:3, cols] * u_ref[8:8 + tm, :])
        gate = c[:, 0:FFN_COLS]
        h_refs[j % FFN_SLOTS][...] = (gate * _sigmoid(gate) * c[:, FFN_COLS:bw]).astype(BF16)

    def down(j):
        part = jnp.dot(h_refs[j % FFN_SLOTS][...], wdn_ref[j * FFN_COLS:(j + 1) * FFN_COLS, :],
                       preferred_element_type=F32)
        if j == 0:
            acc_ref[...] = part
        else:
            acc_ref[...] += part

    n_blocks = D_FF // FFN_COLS
    for j in range(min(FFN_LOOKAHEAD, n_blocks)):
        up(j)
    for j in range(n_blocks + 1):
        if j < n_blocks:
            if j + FFN_LOOKAHEAD < n_blocks:
                up(j + FFN_LOOKAHEAD)
            gated(j)
        if j >= 1:
            down(j - 1)
    o_ref[...] = _layer_norm(ALPHA * x1 + acc_ref[...], ln2g_ref[...], ln2b_ref[...])


def _out_ffn(y, x, wout, ln1g, ln1b, wup, cw, cb, wdn, ln2g, ln2b):
    batch, seq, _ = x.shape
    tm = min(ROW_TILE, seq)
    rows = lambda b, s: (b, s, 0)
    c2 = lambda b, s: (0, 0)
    one = pl.Buffered(1)
    return pl.pallas_call(
        _ffn_kernel,
        grid=(batch, seq // tm),
        in_specs=[
            pl.BlockSpec((None, tm, D_MODEL), rows),
            pl.BlockSpec((None, tm, D_MODEL), rows),
            pl.BlockSpec((D_MODEL, D_MODEL), c2, pipeline_mode=one),
            pl.BlockSpec((1, D_MODEL), c2),
            pl.BlockSpec((1, D_MODEL), c2),
            pl.BlockSpec((D_MODEL, 2 * D_FF), c2, pipeline_mode=one),
            pl.BlockSpec((CONV_W, 2 * D_FF), c2),
            pl.BlockSpec((1, 2 * D_FF), c2),
            pl.BlockSpec((D_FF, D_MODEL), c2, pipeline_mode=one),
            pl.BlockSpec((1, D_MODEL), c2),
            pl.BlockSpec((1, D_MODEL), c2),
        ],
        out_specs=pl.BlockSpec((None, tm, D_MODEL), rows),
        out_shape=jax.ShapeDtypeStruct((batch, seq, D_MODEL), F32),
        scratch_shapes=[pltpu.VMEM((8, 2 * D_FF), F32),
                        pltpu.VMEM((tm, D_MODEL), F32),
                        pltpu.VMEM((tm, D_MODEL), BF16),
                        *[pltpu.VMEM((tm + 8, 2 * FFN_COLS), F32) for _ in range(FFN_SLOTS)],
                        *[pltpu.VMEM((tm, FFN_COLS), BF16) for _ in range(FFN_SLOTS)]],
        compiler_params=pltpu.CompilerParams(dimension_semantics=("parallel", "arbitrary"),
                                             vmem_limit_bytes=VMEM_LIMIT),
        name="out_ffn",
    )(y, x, wout, ln1g, ln1b, wup, cw, cb, wdn, ln2g, ln2b)


def _hi_lo(w):
    hi = w.astype(BF16)
    return hi, (w - hi.astype(F32)).astype(BF16)


def _mixer_constants(seq):
    half = HEAD_DIM // 2
    lane = jnp.arange(PAIR)
    inv = ROPE_BASE ** (-jnp.arange(half, dtype=F32) / half)
    ang = jnp.arange(seq, dtype=F32)[:, None] * inv[None, :]
    idx = lane % half
    cos_t = jnp.cos(ang)[:, idx]
    sign = jnp.where((lane % HEAD_DIM) < half, -1.0, 1.0).astype(F32)
    sin_t = jnp.sin(ang)[:, idx] * sign[None, :]

    log_g = jnp.log(1.0 - jnp.exp2(-5.0 - jnp.arange(RET_HEADS, dtype=F32)))
    t = jnp.arange(CHUNK, dtype=F32)
    row_t = jnp.arange(PAIR) % CHUNK
    row_h = jnp.arange(PAIR) // CHUNK
    pairs = jnp.arange(RET_HEADS // 2)
    lg_rows = log_g[2 * pairs[:, None] + row_h[None, :]]
    same = (row_h[:, None] == row_h[None, :])
    dist = jnp.abs(row_t[:, None] - row_t[None, :]).astype(F32)
    dmat = jnp.where(same[None], jnp.exp(lg_rows[:, :, None] * dist[None]), 0.0)
    qdec = jnp.broadcast_to(jnp.exp(lg_rows * (row_t.astype(F32) + 1.0)[None, :])[:, :, None],
                            (RET_HEADS // 2, PAIR, PAIR))
    lg_lane = jnp.repeat(log_g, HEAD_DIM)
    kdec = jnp.exp(lg_lane[None, :] * (CHUNK - 1.0 - t)[:, None])
    chunk_decay = jnp.exp(lg_lane * CHUNK)
    return (cos_t, sin_t, dmat.astype(F32), qdec.astype(F32), kdec.astype(F32)), chunk_decay


def _attention_bias(rel_bias):
    i = jnp.arange(CHUNK)
    j = jnp.arange(BAND)
    rel = (i[:, None] + BAND_PAD) - j[None, :]
    rel_idx = jnp.clip(rel, -(CHUNK - 1), REL_CLIP) + (CHUNK - 1)
    bias = rel_bias.astype(F32)[:, rel_idx]
    return bias.reshape(ATT_HEADS // 2, PAIR, BAND)


def kernel(x, ln_in_g, ln_in_b, w_in, rw_mu, rw_w0, rw_w_up, rw_a0, rw_a_up, rw_g_up, rw_k_k, rw_k_a, rw_r_k, rw_ln_g, rw_ln_b, ret_gn_g, ret_gn_b, attn_rel_bias, w_out, ln1_g, ln1_b, ffn_w_up, ffn_conv_w, ffn_conv_b, ffn_w_down, ln2_g, ln2_b):
    batch, seq, _ = x.shape
    consts, chunk_decay = _mixer_constants(seq)
    row2 = lambda t: t.reshape(1, -1).astype(F32)
    blocks = jnp.arange(D_FF).reshape(D_FF // FFN_COLS, FFN_COLS)
    ffn_perm = jnp.concatenate([blocks, blocks + D_FF], axis=1).reshape(-1)
    xcur = x.reshape(batch * seq, D_MODEL)
    for l in range(DEPTH):
        if l == 0:
            z, xcur = _project(xcur, row2(ln_in_g), row2(ln_in_b), w_in[l].astype(BF16), True)
        else:
            (z,) = _project(xcur, row2(ln_in_g), row2(ln_in_b), w_in[l].astype(BF16), False)

        zero_row = jnp.zeros((RWKV_W,), F32)
        rwv = jnp.stack([rw_w0[l], rw_a0[l], rw_k_k[l], rw_k_a[l], rw_r_k[l].reshape(-1),
                         rw_ln_g[l], rw_ln_b[l], zero_row]).astype(F32)
        wwa = jnp.zeros((PAIR, 2 * RWKV_W), F32)
        wwa = wwa.at[0:DECAY_LORA, 0:RWKV_W].set(rw_w_up[l].astype(F32))
        wwa = wwa.at[DECAY_LORA:PAIR, RWKV_W:2 * RWKV_W].set(rw_a_up[l].astype(F32))
        wwa_hi, wwa_lo = _hi_lo(wwa)
        gup_hi, gup_lo = _hi_lo(rw_g_up[l].astype(F32))
        retv = jnp.concatenate([jnp.stack([ret_gn_g[l].astype(F32), ret_gn_b[l].astype(F32), chunk_decay]),
                                jnp.zeros((5, RET_W), F32)])
        mix_params = (row2(rw_mu[l]), rwv, wwa_hi, wwa_lo, gup_hi, gup_lo, retv,
                      _attention_bias(attn_rel_bias[l]))
        y = _mixers(z.reshape(batch, seq, N_IN_COLS), mix_params, consts, batch, seq)

        xcur = _out_ffn(y, xcur.reshape(batch, seq, D_MODEL), w_out[l].astype(BF16),
                        row2(ln1_g[l]), row2(ln1_b[l]), ffn_w_up[l][:, ffn_perm].astype(BF16),
                        ffn_conv_w[l][:, ffn_perm].astype(F32), row2(ffn_conv_b[l][ffn_perm]),
                        ffn_w_down[l].astype(BF16),
                        row2(ln2_g[l]), row2(ln2_b[l])).reshape(batch * seq, D_MODEL)
    return xcur.reshape(batch, seq, D_MODEL)
```

```python
import functools

import jax
import jax.numpy as jnp
from jax import lax
from jax.experimental import pallas as pl
from jax.experimental.pallas import tpu as pltpu

F32 = jnp.float32
BF16 = jnp.bfloat16

D_MODEL = 1024
DEPTH = 2
CHUNK = 64
HEAD_DIM = 64
PAIR = 2 * HEAD_DIM
RWKV_HEADS, RET_HEADS, ATT_HEADS = 6, 6, 4
RWKV_W, RET_W, ATT_W = 384, 384, 256
DECAY_LORA, AAA_LORA, GATE_LORA = 64, 64, 128
RWKV_COLS = 3 * RWKV_W + DECAY_LORA + AAA_LORA + GATE_LORA
RET_COLS = 4 * RET_W
ATT_COLS = 3 * ATT_W
N_IN_COLS = RWKV_COLS + RET_COLS + ATT_COLS
RET_OFF = RWKV_COLS
ATT_OFF = RWKV_COLS + RET_COLS
BAND_PREV_CHUNKS = 8
BAND_PAD = BAND_PREV_CHUNKS * CHUNK
BAND = BAND_PAD + CHUNK
REL_CLIP = 128
D_FF = 2816
CONV_W = 3
ALPHA = (2 * DEPTH) ** 0.25
ROPE_BASE = 10000.0
LN_EPS = 1e-5
RWKV_GN_EPS = 64e-5
RET_GN_EPS = 1e-5

ROW_TILE = 512
FFN_COLS = 256
FFN_LOOKAHEAD = 2
FFN_SLOTS = FFN_LOOKAHEAD + 1
MIX_BATCH = 4
MIX_GROUP_A = 4
PREP_FILL = 2
PREP_PIECES_PER_STAGE = 2
VMEM_LIMIT = 56 * 1024 * 1024


def _dot(a, b):
    return jnp.dot(a.astype(BF16), b.astype(BF16), preferred_element_type=F32)


def _dot_nt(a, b):
    return lax.dot_general(a.astype(BF16), b.astype(BF16), (((1,), (1,)), ((), ())),
                           preferred_element_type=F32)


def _dot_tn(a, b):
    return lax.dot_general(a.astype(BF16), b.astype(BF16), (((0,), (0,)), ((), ())),
                           preferred_element_type=F32)


def _split2(x):
    hi = x.astype(BF16)
    lo = (x - hi.astype(F32)).astype(BF16)
    return hi, lo


def _split3(x):
    hi = x.astype(BF16)
    r1 = x - hi.astype(F32)
    mid = r1.astype(BF16)
    lo = (r1 - mid.astype(F32)).astype(BF16)
    return hi, mid, lo


def _dot_x3(x, w_hi, w_lo):
    x_hi, x_lo = _split2(x)
    return (jnp.dot(x_hi, w_hi, preferred_element_type=F32)
            + jnp.dot(x_hi, w_lo, preferred_element_type=F32)
            + jnp.dot(x_lo, w_hi, preferred_element_type=F32))


def _layer_norm(x, g, b, eps=LN_EPS):
    mu = jnp.mean(x, axis=-1, keepdims=True)
    d = x - mu
    var = jnp.mean(d * d, axis=-1, keepdims=True)
    return d * lax.rsqrt(var + eps) * g + b


def _sigmoid(x):
    return 1.0 / (1.0 + jnp.exp(-x))


def _proj_kernel(apply_ln, x_ref, g_ref, b_ref, w_ref, z_ref, *xn_ref):
    x = x_ref[...]
    if apply_ln:
        x = _layer_norm(x, g_ref[...], b_ref[...])
        xn_ref[0][...] = x
    z_ref[...] = jnp.dot(x.astype(BF16), w_ref[...], preferred_element_type=F32)


def _project(x2d, ln_g, ln_b, w_bf16, apply_ln):
    m = x2d.shape[0]
    tm = ROW_TILE
    row = lambda i: (i, 0)
    const = lambda i: (0, 0)
    out_shape = [jax.ShapeDtypeStruct((m, N_IN_COLS), F32)]
    out_specs = [pl.BlockSpec((tm, N_IN_COLS), row)]
    if apply_ln:
        out_shape.append(jax.ShapeDtypeStruct((m, D_MODEL), F32))
        out_specs.append(pl.BlockSpec((tm, D_MODEL), row))
    outs = pl.pallas_call(
        functools.partial(_proj_kernel, apply_ln),
        grid=(m // tm,),
        in_specs=[pl.BlockSpec((tm, D_MODEL), row),
                  pl.BlockSpec((1, D_MODEL), const),
                  pl.BlockSpec((1, D_MODEL), const),
                  pl.BlockSpec((D_MODEL, N_IN_COLS), const, pipeline_mode=pl.Buffered(1))],
        out_specs=out_specs,
        out_shape=out_shape,
        compiler_params=pltpu.CompilerParams(dimension_semantics=("parallel",),
                                             vmem_limit_bytes=VMEM_LIMIT),
        name="in_proj",
    )(x2d, ln_g, ln_b, w_bf16)
    return outs


def _mixer_kernel(z_ref, mu_ref, rwv_ref, wwa_hi_ref, wwa_lo_ref, gup_ref,
                  retv_ref, cos_ref, sin_ref, dmat_ref, qdec_ref, kdec_ref, bias_ref,
                  y_ref, s_ref, r_ref, carry_ref, kbuf_ref, vbuf_ref):
    n = pl.program_id(1)
    C = CHUNK

    @pl.when(n == 0)
    def _():
        s_ref[...] = jnp.zeros_like(s_ref)
        r_ref[...] = jnp.zeros_like(r_ref)
        carry_ref[...] = jnp.zeros_like(carry_ref)
        kbuf_ref[:, :, 0:BAND_PAD, :] = jnp.zeros((MIX_BATCH, ATT_HEADS // 2, BAND_PAD, PAIR), BF16)
        vbuf_ref[:, :, 0:BAND_PAD, :] = jnp.zeros((MIX_BATCH, ATT_HEADS // 2, BAND_PAD, PAIR), BF16)

    lane = lax.broadcasted_iota(jnp.int32, (C, PAIR), 1)
    head0 = lane < HEAD_DIM

    def blk(x):
        return jnp.concatenate([jnp.where(head0, x, 0.0), jnp.where(head0, 0.0, x)], axis=0)

    def unblk(xb):
        return xb[0:C] + xb[C:2 * C]

    rr = lax.broadcasted_iota(jnp.int32, (PAIR, PAIR), 0)
    cc = lax.broadcasted_iota(jnp.int32, (PAIR, PAIR), 1)
    tril_incl = rr >= cc
    tril_strict = rr > cc
    r2 = lax.broadcasted_iota(jnp.int32, (2 * PAIR, 2 * PAIR), 0)
    c2 = lax.broadcasted_iota(jnp.int32, (2 * PAIR, 2 * PAIR), 1)
    head_ones = ((r2 // HEAD_DIM) == (c2 // HEAD_DIM)).astype(BF16)

    def head_sums(slabs):
        x = jnp.concatenate([jnp.concatenate(slabs[i:i + 2], axis=1) for i in range(0, len(slabs), 2)], axis=0)
        s = jnp.dot(x.astype(BF16), head_ones, preferred_element_type=F32)
        return [s[(i // 2) * C:(i // 2 + 1) * C, (i % 2) * PAIR:(i % 2 + 1) * PAIR] for i in range(len(slabs))]

    def head_norms(ys, gs, bs, epss):
        mus = head_sums(ys)
        ds = [y - mu * (1.0 / HEAD_DIM) for y, mu in zip(ys, mus)]
        vs = head_sums([d * d for d in ds])
        return [d * lax.rsqrt(var * (1.0 / HEAD_DIM) + eps) * g + b
                for d, var, g, b, eps in zip(ds, vs, gs, bs, epss)]

    w0 = rwv_ref[0:1, :]
    a0 = rwv_ref[1:2, :]
    k_k = rwv_ref[2:3, :]
    k_a = rwv_ref[3:4, :]
    r_k = rwv_ref[4:5, :]
    ln_g = rwv_ref[5:6, :]
    ln_b = rwv_ref[6:7, :]
    gn_g = retv_ref[0:1, :]
    gn_b = retv_ref[1:2, :]
    chunk_decay = retv_ref[2:3, :]
    cos_t = cos_ref[...]
    sin_t = sin_ref[...]
    first_half = (lane % HEAD_DIM) < (HEAD_DIM // 2)
    row = lax.broadcasted_iota(jnp.int32, (C, RWKV_COLS), 0)
    tri = (lax.broadcasted_iota(jnp.int32, (C, C), 0)
           >= lax.broadcasted_iota(jnp.int32, (C, C), 1)).astype(BF16)
    off = pl.multiple_of(n * C, C)
    col = lax.broadcasted_iota(jnp.int32, (PAIR, BAND), 1)
    valid = col >= BAND_PAD - n * C
    pairs = [slice(p * PAIR, (p + 1) * PAIR) for p in range(RWKV_HEADS // 2)]

    def rope(t):
        partner = jnp.where(first_half, pltpu.roll(t, PAIR - HEAD_DIM // 2, 1),
                            pltpu.roll(t, HEAD_DIM // 2, 1))
        return t * cos_t + partner * sin_t

    def prepare(i, sink):
        zc = z_ref[i, :, 0:RWKV_COLS]
        prev = jnp.where(row == 0, carry_ref[i, 7:8, :], pltpu.roll(zc, 1, 0))
        carry_ref[i] = zc[C - 8:C, :]
        zs = zc + (prev - zc) * mu_ref[...]
        r = zs[:, 0:RWKV_W]
        k = zs[:, RWKV_W:2 * RWKV_W]
        v = zs[:, 2 * RWKV_W:3 * RWKV_W]
        wa_l = zs[:, 3 * RWKV_W:3 * RWKV_W + PAIR]
        g_l = zs[:, 3 * RWKV_W + PAIR:RWKV_COLS]

        wa_in = jnp.where(head0, jnp.tanh(wa_l), wa_l)
        lora = _dot_x3(wa_in, wwa_hi_ref[...], wwa_lo_ref[...])
        w_pre = w0 + lora[:, 0:RWKV_W]
        a = _sigmoid(a0 + lora[:, RWKV_W:2 * RWKV_W])
        g = jnp.dot(_sigmoid(g_l).astype(BF16), gup_ref[...], preferred_element_type=F32)
        sp = jnp.maximum(-w_pre, 0.0) + jnp.log(1.0 + jnp.exp(-jnp.abs(w_pre)))
        log_decay = -jnp.exp(-sp - 0.5)
        yield

        ld3 = jnp.concatenate(_split3(log_decay), axis=1)
        cs = jnp.dot(tri, ld3, preferred_element_type=F32)
        cum = cs[:, 0:RWKV_W] + cs[:, RWKV_W:2 * RWKV_W] + cs[:, 2 * RWKV_W:3 * RWKV_W]
        e_incl = jnp.exp(cum)
        e_excl = jnp.exp(cum - log_decay)
        e_inv = jnp.exp(-cum)
        yield

        kk = k * k_k
        k2 = k * (1.0 + (a - 1.0) * k_a)
        rkr = r * k2 * r_k
        sums = head_sums([kk[:, sl] * kk[:, sl] for sl in pairs] + [rkr[:, sl] for sl in pairs])
        kk_ss, rkr_sum = sums[0:3], sums[3:6]
        sink["vals"] = dict(v=v, g=g, rkr_sum=rkr_sum)
        sink["chains"] = chains = []
        yield

        for p, sl in enumerate(pairs):
            kk_p = kk[:, sl] / jnp.maximum(jnp.sqrt(kk_ss[p]), 1e-12)
            a_hat = blk(-kk_p * e_excl[:, sl])
            r_hat = blk(r[:, sl] * e_incl[:, sl])
            b_til = blk(kk_p * a[:, sl] * e_inv[:, sl])
            k_til = blk(k2[:, sl] * e_inv[:, sl])
            chains.append(dict(
                i=i, p=p,
                ar=jnp.concatenate([a_hat, r_hat], axis=0).astype(BF16),
                bk=jnp.concatenate([b_til, k_til], axis=0).astype(BF16),
                v_b=blk(v[:, sl]).astype(BF16),
                p_end=e_incl[C - 1:C, sl]))
            yield

    fillers = []
    y_ret = {}
    gates = {}
    y_att = {}

    def retention_tasks(i, p):
        st = {}
        sl = pairs[p]
        base = RET_OFF + p * PAIR

        def scores():
            q = rope(z_ref[i, :, base:base + PAIR])
            kr = rope(z_ref[i, :, base + RET_W:base + RET_W + PAIR]) * (HEAD_DIM ** -0.5)
            gg = z_ref[i, :, base + 3 * RET_W:base + 3 * RET_W + PAIR]
            gates[i, p] = gg * _sigmoid(gg)
            st["q_b"] = blk(q)
            st["kd_b"] = blk(kr * kdec_ref[:, sl])
            st["v_b"] = blk(z_ref[i, :, base + 2 * RET_W:base + 2 * RET_W + PAIR]).astype(BF16)
            st["scores"] = _dot_nt(st["q_b"], blk(kr)) * dmat_ref[p]

        def output():
            r_old = r_ref[i, p]
            lhs = jnp.concatenate([st["scores"], st["q_b"] * qdec_ref[p]], axis=1)
            rhs = jnp.concatenate([st["v_b"], r_old.astype(BF16)], axis=0)
            y_ret[i, p] = unblk(_dot(lhs, rhs))
            r_ref[i, p] = r_old * chunk_decay[:, sl] + _dot_tn(st["kd_b"], st["v_b"])

        return [scores, output]

    def attention_tasks(i, p):
        st = {}
        base = ATT_OFF + p * PAIR

        def scores():
            q = z_ref[i, :, base:base + PAIR] * (HEAD_DIM ** -0.5)
            kbuf_ref[i, p, pl.ds(BAND_PAD + off, C), :] = (
                z_ref[i, :, base + ATT_W:base + ATT_W + PAIR].astype(BF16))
            vbuf_ref[i, p, pl.ds(BAND_PAD + off, C), :] = (
                z_ref[i, :, base + 2 * ATT_W:base + 2 * ATT_W + PAIR].astype(BF16))
            k_band = kbuf_ref[i, p, pl.ds(off, BAND), :]
            s = _dot_nt(blk(q), k_band) + bias_ref[p]
            s = jnp.where(valid, s, -1e30)
            m = jnp.max(s, axis=-1, keepdims=True)
            st["e"] = jnp.exp(s - m)
            st["l"] = jnp.sum(st["e"], axis=-1, keepdims=True)

        def output():
            v_band = vbuf_ref[i, p, pl.ds(off, BAND), :]
            o = _dot(st["e"], v_band) / st["l"]
            y_att[i, p] = jnp.where(head0, o[0:C], o[C:2 * C])

        return [scores, output]

    y_rwkv = {}

    def chain_stages(chains):
        for c in chains:
            c["s_old"] = s_ref[c["i"], c["p"]]
            big = _dot_nt(c["ar"], c["bk"])
            c["pw"] = jnp.where(tril_strict, big[0:PAIR, 0:PAIR], 0.0).astype(BF16)
            c["a_ak"] = jnp.where(tril_strict, big[0:PAIR, PAIR:2 * PAIR], 0.0)
            c["l_rbk"] = jnp.concatenate(
                [jnp.where(tril_incl, big[PAIR:2 * PAIR, 0:PAIR], 0.0),
                 jnp.where(tril_incl, big[PAIR:2 * PAIR, PAIR:2 * PAIR], 0.0)], axis=1).astype(BF16)
        yield
        for c in chains:
            c["ars"] = _dot_nt(c["ar"], c["s_old"])
        yield
        for c in chains:
            c["x"] = c["ars"][0:PAIR] + _dot(c["a_ak"], c["v_b"])
        yield
        for _ in range(5):
            for c in chains:
                res = jnp.dot(c["pw"], jnp.concatenate([c["x"].astype(BF16), c["pw"]], axis=1),
                              preferred_element_type=F32)
                c["x"] = c["x"] + res[:, 0:PAIR]
                c["pw"] = res[:, PAIR:2 * PAIR].astype(BF16)
            yield
        for c in chains:
            c["x"] = c["x"] + jnp.dot(c["pw"], c["x"].astype(BF16), preferred_element_type=F32)
            c["uv"] = jnp.concatenate([c["x"].astype(BF16), c["v_b"]], axis=0)
        yield
        for c in chains:
            y_b = c["ars"][PAIR:2 * PAIR] + jnp.dot(c["l_rbk"], c["uv"], preferred_element_type=F32)
            y_rwkv[c["i"], c["p"]] = unblk(y_b)
        yield
        for c in chains:
            s_ref[c["i"], c["p"]] = (c["s_old"] + _dot_tn(c["uv"], c["bk"])) * c["p_end"]
        yield

    seqs = range(MIX_BATCH)
    first, second = [], []
    for i in seqs:
        for p in range(RET_HEADS // 2):
            t = retention_tasks(i, p)
            first.append(t[0])
            second.append(t[1])
        for p in range(ATT_HEADS // 2):
            t = attention_tasks(i, p)
            first.append(t[0])
            second.append(t[1])
    fillers = first + second

    def fill(count):
        for _ in range(count):
            if fillers:
                fillers.pop(0)()

    group_a = list(seqs)[:MIX_GROUP_A]
    group_b = list(seqs)[MIX_GROUP_A:]
    sinks = {i: {} for i in seqs}
    for i in group_a:
        for _ in prepare(i, sinks[i]):
            fill(PREP_FILL)
    prep_b = (None for i in group_b for _ in prepare(i, sinks[i]))
    for _ in chain_stages([c for i in group_a for c in sinks[i]["chains"]]):
        for _ in range(PREP_PIECES_PER_STAGE):
            next(prep_b, None)
    for _ in prep_b:
        pass
    for _ in chain_stages([c for i in group_b for c in sinks[i]["chains"]]):
        fill(1)
    fill(len(fillers))
    seq_vals = {i: sinks[i]["vals"] for i in seqs}

    normed_all = head_norms(
        [y for i in seqs for y in ([y_rwkv[i, p] for p in range(3)] + [y_ret[i, p] for p in range(3)])],
        ([ln_g[:, sl] for sl in pairs] + [gn_g[:, sl] for sl in pairs]) * MIX_BATCH,
        ([ln_b[:, sl] for sl in pairs] + [gn_b[:, sl] for sl in pairs]) * MIX_BATCH,
        ([RWKV_GN_EPS] * 3 + [RET_GN_EPS] * 3) * MIX_BATCH)
    for i in seqs:
        vals = seq_vals[i]
        normed = normed_all[6 * i:6 * i + 6]
        out = []
        for p, sl in enumerate(pairs):
            out.append((normed[p] + vals["rkr_sum"][p] * vals["v"][:, sl]) * vals["g"][:, sl])
        for p in range(RET_HEADS // 2):
            out.append(normed[3 + p] * gates[i, p])
        for p in range(ATT_HEADS // 2):
            out.append(y_att[i, p])
        y_ref[i] = jnp.concatenate(out, axis=1)


def _mixers(z, mix_params, consts, batch, seq):
    nc = seq // CHUNK
    nb = MIX_BATCH
    c2 = lambda b, n: (0, 0)
    c3 = lambda b, n: (0, 0, 0)
    (mu, rwv, wwa_hi, wwa_lo, gup, retv, bias) = mix_params
    (cos_t, sin_t, dmat, qdec, kdec) = consts
    in_specs = [
        pl.BlockSpec((nb, CHUNK, N_IN_COLS), lambda b, n: (b, n, 0)),
        pl.BlockSpec((1, RWKV_COLS), c2),
        pl.BlockSpec((8, RWKV_W), c2),
        pl.BlockSpec((PAIR, 2 * RWKV_W), c2),
        pl.BlockSpec((PAIR, 2 * RWKV_W), c2),
        pl.BlockSpec((GATE_LORA, RWKV_W), c2),
        pl.BlockSpec((8, RET_W), c2),
        pl.BlockSpec((CHUNK, PAIR), lambda b, n: (n, 0)),
        pl.BlockSpec((CHUNK, PAIR), lambda b, n: (n, 0)),
        pl.BlockSpec((RET_HEADS // 2, PAIR, PAIR), c3),
        pl.BlockSpec((RET_HEADS // 2, PAIR, PAIR), c3),
        pl.BlockSpec((CHUNK, RET_W), c2),
        pl.BlockSpec((ATT_HEADS // 2, PAIR, BAND), c3),
    ]
    return pl.pallas_call(
        _mixer_kernel,
        grid=(batch // nb, nc),
        in_specs=in_specs,
        out_specs=pl.BlockSpec((nb, CHUNK, D_MODEL), lambda b, n: (b, n, 0)),
        out_shape=jax.ShapeDtypeStruct((batch, seq, D_MODEL), F32),
        scratch_shapes=[
            pltpu.VMEM((nb, RWKV_HEADS // 2, PAIR, PAIR), F32),
            pltpu.VMEM((nb, RET_HEADS // 2, PAIR, PAIR), F32),
            pltpu.VMEM((nb, 8, RWKV_COLS), F32),
            pltpu.VMEM((nb, ATT_HEADS // 2, BAND_PAD + seq, PAIR), BF16),
            pltpu.VMEM((nb, ATT_HEADS // 2, BAND_PAD + seq, PAIR), BF16),
        ],
        compiler_params=pltpu.CompilerParams(dimension_semantics=("parallel", "arbitrary"),
                                             vmem_limit_bytes=VMEM_LIMIT),
        name="mixers",
    )(z, mu, rwv, wwa_hi, wwa_lo, gup, retv, cos_t, sin_t, dmat, qdec, kdec, bias)


def _ffn_kernel(y_ref, x_ref, wout_ref, ln1g_ref, ln1b_ref, wup_ref, cw_ref, cb_ref, wdn_ref,
                ln2g_ref, ln2b_ref, o_ref, carry_ref, acc_ref, x1_ref, *slot_refs):
    tm = x_ref.shape[0]

    @pl.when(pl.program_id(1) == 0)
    def _():
        carry_ref[...] = jnp.zeros_like(carry_ref)

    h = ALPHA * x_ref[...] + jnp.dot(y_ref[...].astype(BF16), wout_ref[...], preferred_element_type=F32)
    x1 = _layer_norm(h, ln1g_ref[...], ln1b_ref[...])
    x1_ref[...] = x1.astype(BF16)
    bw = 2 * FFN_COLS

    u_refs = slot_refs[0:FFN_SLOTS]
    h_refs = slot_refs[FFN_SLOTS:2 * FFN_SLOTS]

    def up(j):
        u_ref = u_refs[j % FFN_SLOTS]
        cols = slice(j * bw, (j + 1) * bw)
        gcols = slice(j * FFN_COLS, (j + 1) * FFN_COLS)
        vcols = slice(D_FF + j * FFN_COLS, D_FF + (j + 1) * FFN_COLS)
        u_ref[0:8, :] = carry_ref[:, cols]
        u_ref[8:8 + tm, 0:FFN_COLS] = jnp.dot(x1_ref[...], wup_ref[:, gcols], preferred_element_type=F32)
        u_ref[8:8 + tm, FFN_COLS:bw] = jnp.dot(x1_ref[...], wup_ref[:, vcols], preferred_element_type=F32)
        carry_ref[:, cols] = u_ref[tm:tm + 8, :]

    def gated(j):
        u_ref = u_refs[j % FFN_SLOTS]
        halves = []
        for part in range(2):
            cols = slice(part * D_FF + j * FFN_COLS, part * D_FF + (j + 1) * FFN_COLS)
            ucols = slice(part * FFN_COLS, (part + 1) * FFN_COLS)
            halves.append(cb_ref[:, cols] + cw_ref[0:1, cols] * u_ref[6:6 + tm, ucols]
                          + cw_ref[1:2, cols] * u_ref[7:7 + tm, ucols]
                          + cw_ref[2:3, cols] * u_ref[8:8 + tm, ucols])
        gate, val = halves
        h_refs[j % FFN_SLOTS][...] = (gate * _sigmoid(gate) * val).astype(BF16)

    def down(j):
        part = jnp.dot(h_refs[j % FFN_SLOTS][...], wdn_ref[j * FFN_COLS:(j + 1) * FFN_COLS, :],
                       preferred_element_type=F32)
        if j == 0:
            acc_ref[...] = part
        else:
            acc_ref[...] += part

    n_blocks = D_FF // FFN_COLS
    for j in range(min(FFN_LOOKAHEAD, n_blocks)):
        up(j)
    for j in range(n_blocks + 1):
        if j < n_blocks:
            if j + FFN_LOOKAHEAD < n_blocks:
                up(j + FFN_LOOKAHEAD)
            gated(j)
        if j >= 1:
            down(j - 1)
    o_ref[...] = _layer_norm(ALPHA * x1 + acc_ref[...], ln2g_ref[...], ln2b_ref[...])


def _out_ffn(y, x, wout, ln1g, ln1b, wup, cw, cb, wdn, ln2g, ln2b):
    batch, seq, _ = x.shape
    tm = min(ROW_TILE, seq)
    rows = lambda b, s: (b, s, 0)
    c2 = lambda b, s: (0, 0)
    one = pl.Buffered(1)
    return pl.pallas_call(
        _ffn_kernel,
        grid=(batch, seq // tm),
        in_specs=[
            pl.BlockSpec((None, tm, D_MODEL), rows),
            pl.BlockSpec((None, tm, D_MODEL), rows),
            pl.BlockSpec((D_MODEL, D_MODEL), c2, pipeline_mode=one),
            pl.BlockSpec((1, D_MODEL), c2),
            pl.BlockSpec((1, D_MODEL), c2),
            pl.BlockSpec((D_MODEL, 2 * D_FF), c2, pipeline_mode=one),
            pl.BlockSpec((CONV_W, 2 * D_FF), c2),
            pl.BlockSpec((1, 2 * D_FF), c2),
            pl.BlockSpec((D_FF, D_MODEL), c2, pipeline_mode=one),
            pl.BlockSpec((1, D_MODEL), c2),
            pl.BlockSpec((1, D_MODEL), c2),
        ],
        out_specs=pl.BlockSpec((None, tm, D_MODEL), rows),
        out_shape=jax.ShapeDtypeStruct((batch, seq, D_MODEL), F32),
        scratch_shapes=[pltpu.VMEM((8, 2 * D_FF), F32),
                        pltpu.VMEM((tm, D_MODEL), F32),
                        pltpu.VMEM((tm, D_MODEL), BF16),
                        *[pltpu.VMEM((tm + 8, 2 * FFN_COLS), F32) for _ in range(FFN_SLOTS)],
                        *[pltpu.VMEM((tm, FFN_COLS), BF16) for _ in range(FFN_SLOTS)]],
        compiler_params=pltpu.CompilerParams(dimension_semantics=("parallel", "arbitrary"),
                                             vmem_limit_bytes=VMEM_LIMIT),
        name="out_ffn",
    )(y, x, wout, ln1g, ln1b, wup, cw, cb, wdn, ln2g, ln2b)


def _hi_lo(w):
    hi = w.astype(BF16)
    return hi, (w - hi.astype(F32)).astype(BF16)


def _mixer_constants(seq):
    half = HEAD_DIM // 2
    lane = jnp.arange(PAIR)
    inv = ROPE_BASE ** (-jnp.arange(half, dtype=F32) / half)
    ang = jnp.arange(seq, dtype=F32)[:, None] * inv[None, :]
    idx = lane % half
    cos_t = jnp.cos(ang)[:, idx]
    sign = jnp.where((lane % HEAD_DIM) < half, -1.0, 1.0).astype(F32)
    sin_t = jnp.sin(ang)[:, idx] * sign[None, :]

    log_g = jnp.log(1.0 - jnp.exp2(-5.0 - jnp.arange(RET_HEADS, dtype=F32)))
    t = jnp.arange(CHUNK, dtype=F32)
    row_t = jnp.arange(PAIR) % CHUNK
    row_h = jnp.arange(PAIR) // CHUNK
    pairs = jnp.arange(RET_HEADS // 2)
    lg_rows = log_g[2 * pairs[:, None] + row_h[None, :]]
    same = (row_h[:, None] == row_h[None, :])
    dist = jnp.abs(row_t[:, None] - row_t[None, :]).astype(F32)
    dmat = jnp.where(same[None], jnp.exp(lg_rows[:, :, None] * dist[None]), 0.0)
    qdec = jnp.broadcast_to(jnp.exp(lg_rows * (row_t.astype(F32) + 1.0)[None, :])[:, :, None],
                            (RET_HEADS // 2, PAIR, PAIR))
    lg_lane = jnp.repeat(log_g, HEAD_DIM)
    kdec = jnp.exp(lg_lane[None, :] * (CHUNK - 1.0 - t)[:, None])
    chunk_decay = jnp.exp(lg_lane * CHUNK)
    return (cos_t, sin_t, dmat.astype(F32), qdec.astype(F32), kdec.astype(F32)), chunk_decay


def _attention_bias(rel_bias):
    i = jnp.arange(CHUNK)
    j = jnp.arange(BAND)
    rel = (i[:, None] + BAND_PAD) - j[None, :]
    rel_idx = jnp.clip(rel, -(CHUNK - 1), REL_CLIP) + (CHUNK - 1)
    bias = rel_bias.astype(F32)[:, rel_idx]
    return bias.reshape(ATT_HEADS // 2, PAIR, BAND)


def kernel(x, ln_in_g, ln_in_b, w_in, rw_mu, rw_w0, rw_w_up, rw_a0, rw_a_up, rw_g_up, rw_k_k, rw_k_a, rw_r_k, rw_ln_g, rw_ln_b, ret_gn_g, ret_gn_b, attn_rel_bias, w_out, ln1_g, ln1_b, ffn_w_up, ffn_conv_w, ffn_conv_b, ffn_w_down, ln2_g, ln2_b):
    batch, seq, _ = x.shape
    consts, chunk_decay = _mixer_constants(seq)
    row2 = lambda t: t.reshape(1, -1).astype(F32)
    xcur = x.reshape(batch * seq, D_MODEL)
    for l in range(DEPTH):
        if l == 0:
            z, xcur = _project(xcur, row2(ln_in_g), row2(ln_in_b), w_in[l].astype(BF16), True)
        else:
            (z,) = _project(xcur, row2(ln_in_g), row2(ln_in_b), w_in[l].astype(BF16), False)

        zero_row = jnp.zeros((RWKV_W,), F32)
        rwv = jnp.stack([rw_w0[l], rw_a0[l], rw_k_k[l], rw_k_a[l], rw_r_k[l].reshape(-1),
                         rw_ln_g[l], rw_ln_b[l], zero_row]).astype(F32)
        wwa = jnp.zeros((PAIR, 2 * RWKV_W), F32)
        wwa = wwa.at[0:DECAY_LORA, 0:RWKV_W].set(rw_w_up[l].astype(F32))
        wwa = wwa.at[DECAY_LORA:PAIR, RWKV_W:2 * RWKV_W].set(rw_a_up[l].astype(F32))
        wwa_hi, wwa_lo = _hi_lo(wwa)
        retv = jnp.concatenate([jnp.stack([ret_gn_g[l].astype(F32), ret_gn_b[l].astype(F32), chunk_decay]),
                                jnp.zeros((5, RET_W), F32)])
        mix_params = (row2(rw_mu[l]), rwv, wwa_hi, wwa_lo, rw_g_up[l].astype(BF16), retv,
                      _attention_bias(attn_rel_bias[l]))
        y = _mixers(z.reshape(batch, seq, N_IN_COLS), mix_params, consts, batch, seq)

        xcur = _out_ffn(y, xcur.reshape(batch, seq, D_MODEL), w_out[l].astype(BF16),
                        row2(ln1_g[l]), row2(ln1_b[l]), ffn_w_up[l].astype(BF16),
                        ffn_conv_w[l].astype(F32), row2(ffn_conv_b[l]), ffn_w_down[l].astype(BF16),
                        row2(ln2_g[l]), row2(ln2_b[l])).reshape(batch * seq, D_MODEL)
    return xcur.reshape(batch, seq, D_MODEL)
```

```python
import functools

import jax
import jax.numpy as jnp
from jax import lax
from jax.experimental import pallas as pl
from jax.experimental.pallas import tpu as pltpu

F32 = jnp.float32
BF16 = jnp.bfloat16

D_MODEL = 1024
DEPTH = 2
CHUNK = 64
HEAD_DIM = 64
PAIR = 2 * HEAD_DIM
RWKV_HEADS, RET_HEADS, ATT_HEADS = 6, 6, 4
RWKV_W, RET_W, ATT_W = 384, 384, 256
DECAY_LORA, AAA_LORA, GATE_LORA = 64, 64, 128
RWKV_COLS = 3 * RWKV_W + DECAY_LORA + AAA_LORA + GATE_LORA
RET_COLS = 4 * RET_W
ATT_COLS = 3 * ATT_W
N_IN_COLS = RWKV_COLS + RET_COLS + ATT_COLS
RET_OFF = RWKV_COLS
ATT_OFF = RWKV_COLS + RET_COLS
BAND_PREV_CHUNKS = 8
BAND_PAD = BAND_PREV_CHUNKS * CHUNK
BAND = BAND_PAD + CHUNK
REL_CLIP = 128
D_FF = 2816
CONV_W = 3
ALPHA = (2 * DEPTH) ** 0.25
ROPE_BASE = 10000.0
LN_EPS = 1e-5
RWKV_GN_EPS = 64e-5
RET_GN_EPS = 1e-5

ROW_TILE = 512
FFN_COLS = 256
FFN_LOOKAHEAD = 2
FFN_SLOTS = FFN_LOOKAHEAD + 1
MIX_BATCH = 4
PREP_FILL = 2
VMEM_LIMIT = 56 * 1024 * 1024

assert CHUNK == HEAD_DIM


def _dot(a, b):
    return jnp.dot(a.astype(BF16), b.astype(BF16), preferred_element_type=F32)


def _dot_nt(a, b):
    return lax.dot_general(a.astype(BF16), b.astype(BF16), (((1,), (1,)), ((), ())),
                           preferred_element_type=F32)


def _dot_tn(a, b):
    return lax.dot_general(a.astype(BF16), b.astype(BF16), (((0,), (0,)), ((), ())),
                           preferred_element_type=F32)


def _split2(x):
    hi = x.astype(BF16)
    lo = (x - hi.astype(F32)).astype(BF16)
    return hi, lo


def _split3(x):
    hi = x.astype(BF16)
    r1 = x - hi.astype(F32)
    mid = r1.astype(BF16)
    lo = (r1 - mid.astype(F32)).astype(BF16)
    return hi, mid, lo


def _dot_x3(x, w_hi, w_lo):
    x_hi, x_lo = _split2(x)
    return (jnp.dot(x_hi, w_hi, preferred_element_type=F32)
            + jnp.dot(x_hi, w_lo, preferred_element_type=F32)
            + jnp.dot(x_lo, w_hi, preferred_element_type=F32))


def _layer_norm(x, g, b, eps=LN_EPS):
    mu = jnp.mean(x, axis=-1, keepdims=True)
    d = x - mu
    var = jnp.mean(d * d, axis=-1, keepdims=True)
    return d * lax.rsqrt(var + eps) * g + b


def _sigmoid(x):
    return 1.0 / (1.0 + jnp.exp(-x))


def _proj_kernel(apply_ln, x_ref, g_ref, b_ref, w_ref, z_ref, *xn_ref):
    x = x_ref[...]
    if apply_ln:
        x = _layer_norm(x, g_ref[...], b_ref[...])
        xn_ref[0][...] = x
    z_ref[...] = jnp.dot(x.astype(BF16), w_ref[...], preferred_element_type=F32)


def _project(x2d, ln_g, ln_b, w_bf16, apply_ln):
    m = x2d.shape[0]
    tm = ROW_TILE
    row = lambda i: (i, 0)
    const = lambda i: (0, 0)
    out_shape = [jax.ShapeDtypeStruct((m, N_IN_COLS), F32)]
    out_specs = [pl.BlockSpec((tm, N_IN_COLS), row)]
    if apply_ln:
        out_shape.append(jax.ShapeDtypeStruct((m, D_MODEL), F32))
        out_specs.append(pl.BlockSpec((tm, D_MODEL), row))
    outs = pl.pallas_call(
        functools.partial(_proj_kernel, apply_ln),
        grid=(m // tm,),
        in_specs=[pl.BlockSpec((tm, D_MODEL), row),
                  pl.BlockSpec((1, D_MODEL), const),
                  pl.BlockSpec((1, D_MODEL), const),
                  pl.BlockSpec((D_MODEL, N_IN_COLS), const, pipeline_mode=pl.Buffered(1))],
        out_specs=out_specs,
        out_shape=out_shape,
        compiler_params=pltpu.CompilerParams(dimension_semantics=("parallel",),
                                             vmem_limit_bytes=VMEM_LIMIT),
        name="in_proj",
    )(x2d, ln_g, ln_b, w_bf16)
    return outs


def _mixer_kernel(z_ref, mu_ref, rwv_ref, wwa_hi_ref, wwa_lo_ref, gup_ref,
                  retv_ref, cos_ref, sin_ref, dmat_ref, qdec_ref, kdec_ref, bias_ref,
                  y_ref, s_ref, r_ref, carry_ref, kbuf_ref, vbuf_ref):
    n = pl.program_id(1)
    C = CHUNK

    @pl.when(n == 0)
    def _():
        s_ref[...] = jnp.zeros_like(s_ref)
        r_ref[...] = jnp.zeros_like(r_ref)
        carry_ref[...] = jnp.zeros_like(carry_ref)
        kbuf_ref[:, :, 0:BAND_PAD, :] = jnp.zeros((MIX_BATCH, ATT_HEADS // 2, BAND_PAD, PAIR), BF16)
        vbuf_ref[:, :, 0:BAND_PAD, :] = jnp.zeros((MIX_BATCH, ATT_HEADS // 2, BAND_PAD, PAIR), BF16)

    lane = lax.broadcasted_iota(jnp.int32, (C, PAIR), 1)
    time = lax.broadcasted_iota(jnp.int32, (C, PAIR), 0)
    head0 = lane < HEAD_DIM
    cat_incl = time >= (lane % C)
    cat_strict = time > (lane % C)

    def blk(x):
        zero = jnp.zeros_like(x)
        return jnp.concatenate([jnp.where(head0, x, zero), jnp.where(head0, zero, x)], axis=0)

    def unblk(xb):
        return xb[0:C] + xb[C:2 * C]

    r2 = lax.broadcasted_iota(jnp.int32, (2 * PAIR, 2 * PAIR), 0)
    c2 = lax.broadcasted_iota(jnp.int32, (2 * PAIR, 2 * PAIR), 1)
    head_ones = ((r2 // HEAD_DIM) == (c2 // HEAD_DIM)).astype(BF16)

    def head_sums(slabs):
        x = jnp.concatenate([jnp.concatenate(slabs[i:i + 2], axis=1) for i in range(0, len(slabs), 2)], axis=0)
        s = jnp.dot(x.astype(BF16), head_ones, preferred_element_type=F32)
        return [s[(i // 2) * C:(i // 2 + 1) * C, (i % 2) * PAIR:(i % 2 + 1) * PAIR] for i in range(len(slabs))]

    def head_norms(ys, gs, bs, epss):
        mus = head_sums(ys)
        ds = [y - mu * (1.0 / HEAD_DIM) for y, mu in zip(ys, mus)]
        vs = head_sums([d * d for d in ds])
        return [d * lax.rsqrt(var * (1.0 / HEAD_DIM) + eps) * g + b
                for d, var, g, b, eps in zip(ds, vs, gs, bs, epss)]

    w0 = rwv_ref[0:1, :]
    a0 = rwv_ref[1:2, :]
    k_k = rwv_ref[2:3, :]
    k_a = rwv_ref[3:4, :]
    r_k = rwv_ref[4:5, :]
    ln_g = rwv_ref[5:6, :]
    ln_b = rwv_ref[6:7, :]
    gn_g = retv_ref[0:1, :]
    gn_b = retv_ref[1:2, :]
    chunk_decay = retv_ref[2:3, :]
    cos_t = cos_ref[...]
    sin_t = sin_ref[...]
    first_half = (lane % HEAD_DIM) < (HEAD_DIM // 2)
    row = lax.broadcasted_iota(jnp.int32, (C, RWKV_COLS), 0)
    tri = (lax.broadcasted_iota(jnp.int32, (C, C), 0)
           >= lax.broadcasted_iota(jnp.int32, (C, C), 1)).astype(BF16)
    off = pl.multiple_of(n * C, C)
    col = lax.broadcasted_iota(jnp.int32, (PAIR, BAND), 1)
    valid = col >= BAND_PAD - n * C
    pairs = [slice(p * PAIR, (p + 1) * PAIR) for p in range(RWKV_HEADS // 2)]

    def rope(t):
        partner = jnp.where(first_half, pltpu.roll(t, PAIR - HEAD_DIM // 2, 1),
                            pltpu.roll(t, HEAD_DIM // 2, 1))
        return t * cos_t + partner * sin_t

    def prepare(i, sink):
        zc = z_ref[i, :, 0:RWKV_COLS]
        prev = jnp.where(row == 0, carry_ref[i, 7:8, :], pltpu.roll(zc, 1, 0))
        carry_ref[i] = zc[C - 8:C, :]
        zs = zc + (prev - zc) * mu_ref[...]
        r = zs[:, 0:RWKV_W]
        k = zs[:, RWKV_W:2 * RWKV_W]
        v = zs[:, 2 * RWKV_W:3 * RWKV_W]
        wa_l = zs[:, 3 * RWKV_W:3 * RWKV_W + PAIR]
        g_l = zs[:, 3 * RWKV_W + PAIR:RWKV_COLS]

        wa_in = jnp.where(head0, jnp.tanh(wa_l), wa_l)
        lora = _dot_x3(wa_in, wwa_hi_ref[...], wwa_lo_ref[...])
        w_pre = w0 + lora[:, 0:RWKV_W]
        a = _sigmoid(a0 + lora[:, RWKV_W:2 * RWKV_W])
        g = jnp.dot(_sigmoid(g_l).astype(BF16), gup_ref[...], preferred_element_type=F32)
        sp = jnp.maximum(-w_pre, 0.0) + jnp.log(1.0 + jnp.exp(-jnp.abs(w_pre)))
        log_decay = -jnp.exp(-sp - 0.5)
        yield

        ld3 = jnp.concatenate(_split3(log_decay), axis=1)
        cs = jnp.dot(tri, ld3, preferred_element_type=F32)
        cum = cs[:, 0:RWKV_W] + cs[:, RWKV_W:2 * RWKV_W] + cs[:, 2 * RWKV_W:3 * RWKV_W]
        e_incl = jnp.exp(cum)
        e_excl = jnp.exp(cum - log_decay)
        e_inv = jnp.exp(-cum)
        yield

        kk = k * k_k
        k2 = k * (1.0 + (a - 1.0) * k_a)
        rkr = r * k2 * r_k
        sums = head_sums([kk[:, sl] * kk[:, sl] for sl in pairs] + [rkr[:, sl] for sl in pairs])
        kk_ss, rkr_sum = sums[0:3], sums[3:6]
        sink["vals"] = dict(v=v, g=g, rkr_sum=rkr_sum)
        sink["chains"] = chains = []
        yield

        for p, sl in enumerate(pairs):
            kk_p = kk[:, sl] / jnp.maximum(jnp.sqrt(kk_ss[p]), 1e-12)
            a_hat = -kk_p * e_excl[:, sl]
            r_hat = r[:, sl] * e_incl[:, sl]
            b_til = blk(kk_p * a[:, sl] * e_inv[:, sl])
            k_til = blk(k2[:, sl] * e_inv[:, sl])
            chains.append(dict(
                i=i, p=p,
                ar=jnp.concatenate([a_hat, r_hat], axis=0).astype(BF16),
                bk=jnp.concatenate([b_til, k_til], axis=0).astype(BF16),
                v_b=blk(v[:, sl]).astype(BF16),
                p_end=e_incl[C - 1:C, sl]))
            yield

    y_ret = {}
    gates = {}
    y_att = {}

    def retention_tasks(i, p):
        st = {}
        sl = pairs[p]
        base = RET_OFF + p * PAIR

        def scores():
            q = rope(z_ref[i, :, base:base + PAIR])
            kr = rope(z_ref[i, :, base + RET_W:base + RET_W + PAIR]) * (HEAD_DIM ** -0.5)
            gg = z_ref[i, :, base + 3 * RET_W:base + 3 * RET_W + PAIR]
            gates[i, p] = gg * _sigmoid(gg)
            st["q_dec"] = q * qdec_ref[:, sl]
            st["kd_b"] = blk(kr * kdec_ref[:, sl])
            st["v_b"] = blk(z_ref[i, :, base + 2 * RET_W:base + 2 * RET_W + PAIR]).astype(BF16)
            st["scores"] = _dot_nt(q, blk(kr)) * dmat_ref[p]

        def output():
            r_old = r_ref[i, p]
            lhs = jnp.concatenate([st["scores"], st["q_dec"]], axis=1)
            rhs = jnp.concatenate([st["v_b"], r_old.astype(BF16)], axis=0)
            y_ret[i, p] = _dot(lhs, rhs)
            r_ref[i, p] = r_old * chunk_decay[:, sl] + _dot_tn(st["kd_b"], st["v_b"])

        return [scores, output]

    def attention_tasks(i, p):
        st = {}
        base = ATT_OFF + p * PAIR

        def scores():
            q = z_ref[i, :, base:base + PAIR] * (HEAD_DIM ** -0.5)
            kbuf_ref[i, p, pl.ds(BAND_PAD + off, C), :] = (
                z_ref[i, :, base + ATT_W:base + ATT_W + PAIR].astype(BF16))
            vbuf_ref[i, p, pl.ds(BAND_PAD + off, C), :] = (
                z_ref[i, :, base + 2 * ATT_W:base + 2 * ATT_W + PAIR].astype(BF16))
            k_band = kbuf_ref[i, p, pl.ds(off, BAND), :]
            s = _dot_nt(blk(q), k_band) + bias_ref[p]
            s = jnp.where(valid, s, -1e30)
            m = jnp.max(s, axis=-1, keepdims=True)
            st["e"] = jnp.exp(s - m)
            st["l"] = jnp.sum(st["e"], axis=-1, keepdims=True)

        def output():
            v_band = vbuf_ref[i, p, pl.ds(off, BAND), :]
            o = _dot(st["e"], v_band) / st["l"]
            y_att[i, p] = jnp.where(head0, o[0:C], o[C:2 * C])

        return [scores, output]

    y_rwkv = {}

    def chain_stages(chains):
        for c in chains:
            c["s_old"] = s_ref[c["i"], c["p"]]
            big = _dot_nt(c["ar"], c["bk"])
            c["pw"] = jnp.where(cat_strict, big[0:C, 0:PAIR], 0.0).astype(BF16)
            c["a_ak"] = jnp.where(cat_strict, big[0:C, PAIR:2 * PAIR], 0.0)
            c["l_rbk"] = jnp.concatenate(
                [jnp.where(cat_incl, big[C:2 * C, 0:PAIR], 0.0),
                 jnp.where(cat_incl, big[C:2 * C, PAIR:2 * PAIR], 0.0)], axis=1).astype(BF16)
        yield
        for c in chains:
            c["ars"] = _dot_nt(c["ar"], c["s_old"])
        yield
        for c in chains:
            c["x"] = c["ars"][0:C] + _dot(c["a_ak"], c["v_b"])
        yield
        for _ in range(5):
            for c in chains:
                rhs = jnp.concatenate([blk(c["x"]).astype(BF16), blk(c["pw"])], axis=1)
                res = jnp.dot(c["pw"], rhs, preferred_element_type=F32)
                c["x"] = c["x"] + res[:, 0:PAIR]
                c["pw"] = res[:, PAIR:2 * PAIR].astype(BF16)
            yield
        for c in chains:
            c["x"] = c["x"] + jnp.dot(c["pw"], blk(c["x"]).astype(BF16), preferred_element_type=F32)
            c["uv"] = jnp.concatenate([blk(c["x"]).astype(BF16), c["v_b"]], axis=0)
        yield
        for c in chains:
            y_rwkv[c["i"], c["p"]] = c["ars"][C:2 * C] + jnp.dot(c["l_rbk"], c["uv"], preferred_element_type=F32)
        yield
        for c in chains:
            s_ref[c["i"], c["p"]] = (c["s_old"] + _dot_tn(c["uv"], c["bk"])) * c["p_end"]
        yield

    seqs = range(MIX_BATCH)
    first, second = [], []
    for i in seqs:
        for p in range(RET_HEADS // 2):
            t = retention_tasks(i, p)
            first.append(t[0])
            second.append(t[1])
        for p in range(ATT_HEADS // 2):
            t = attention_tasks(i, p)
            first.append(t[0])
            second.append(t[1])
    fillers = first + second

    def fill(count):
        for _ in range(count):
            if fillers:
                fillers.pop(0)()

    sinks = {i: {} for i in seqs}
    for i in seqs:
        for _ in prepare(i, sinks[i]):
            fill(PREP_FILL)
    for _ in chain_stages([c for i in seqs for c in sinks[i]["chains"]]):
        fill(1)
    fill(len(fillers))
    seq_vals = {i: sinks[i]["vals"] for i in seqs}

    normed_all = head_norms(
        [y for i in seqs for y in ([y_rwkv[i, p] for p in range(3)] + [y_ret[i, p] for p in range(3)])],
        ([ln_g[:, sl] for sl in pairs] + [gn_g[:, sl] for sl in pairs]) * MIX_BATCH,
        ([ln_b[:, sl] for sl in pairs] + [gn_b[:, sl] for sl in pairs]) * MIX_BATCH,
        ([RWKV_GN_EPS] * 3 + [RET_GN_EPS] * 3) * MIX_BATCH)
    for i in seqs:
        vals = seq_vals[i]
        normed = normed_all[6 * i:6 * i + 6]
        out = []
        for p, sl in enumerate(pairs):
            out.append((normed[p] + vals["rkr_sum"][p] * vals["v"][:, sl]) * vals["g"][:, sl])
        for p in range(RET_HEADS // 2):
            out.append(normed[3 + p] * gates[i, p])
        for p in range(ATT_HEADS // 2):
            out.append(y_att[i, p])
        y_ref[i] = jnp.concatenate(out, axis=1)


def _mixers(z, mix_params, consts, batch, seq):
    nc = seq // CHUNK
    nb = MIX_BATCH
    c2 = lambda b, n: (0, 0)
    c3 = lambda b, n: (0, 0, 0)
    (mu, rwv, wwa_hi, wwa_lo, gup, retv, bias) = mix_params
    (cos_t, sin_t, dmat, qdec, kdec) = consts
    in_specs = [
        pl.BlockSpec((nb, CHUNK, N_IN_COLS), lambda b, n: (b, n, 0)),
        pl.BlockSpec((1, RWKV_COLS), c2),
        pl.BlockSpec((8, RWKV_W), c2),
        pl.BlockSpec((PAIR, 2 * RWKV_W), c2),
        pl.BlockSpec((PAIR, 2 * RWKV_W), c2),
        pl.BlockSpec((GATE_LORA, RWKV_W), c2),
        pl.BlockSpec((8, RET_W), c2),
        pl.BlockSpec((CHUNK, PAIR), lambda b, n: (n, 0)),
        pl.BlockSpec((CHUNK, PAIR), lambda b, n: (n, 0)),
        pl.BlockSpec((RET_HEADS // 2, CHUNK, PAIR), c3),
        pl.BlockSpec((CHUNK, RET_W), c2),
        pl.BlockSpec((CHUNK, RET_W), c2),
        pl.BlockSpec((ATT_HEADS // 2, PAIR, BAND), c3),
    ]
    return pl.pallas_call(
        _mixer_kernel,
        grid=(batch // nb, nc),
        in_specs=in_specs,
        out_specs=pl.BlockSpec((nb, CHUNK, D_MODEL), lambda b, n: (b, n, 0)),
        out_shape=jax.ShapeDtypeStruct((batch, seq, D_MODEL), F32),
        scratch_shapes=[
            pltpu.VMEM((nb, RWKV_HEADS // 2, PAIR, PAIR), F32),
            pltpu.VMEM((nb, RET_HEADS // 2, PAIR, PAIR), F32),
            pltpu.VMEM((nb, 8, RWKV_COLS), F32),
            pltpu.VMEM((nb, ATT_HEADS // 2, BAND_PAD + seq, PAIR), BF16),
            pltpu.VMEM((nb, ATT_HEADS // 2, BAND_PAD + seq, PAIR), BF16),
        ],
        compiler_params=pltpu.CompilerParams(dimension_semantics=("parallel", "arbitrary"),
                                             vmem_limit_bytes=VMEM_LIMIT),
        name="mixers",
    )(z, mu, rwv, wwa_hi, wwa_lo, gup, retv, cos_t, sin_t, dmat, qdec, kdec, bias)


def _ffn_kernel(y_ref, x_ref, wout_ref, ln1g_ref, ln1b_ref, wup_ref, cw_ref, cb_ref, wdn_ref,
                ln2g_ref, ln2b_ref, o_ref, carry_ref, acc_ref, x1_ref, *slot_refs):
    tm = x_ref.shape[0]

    @pl.when(pl.program_id(1) == 0)
    def _():
        carry_ref[...] = jnp.zeros_like(carry_ref)

    h = ALPHA * x_ref[...] + jnp.dot(y_ref[...].astype(BF16), wout_ref[...], preferred_element_type=F32)
    x1 = _layer_norm(h, ln1g_ref[...], ln1b_ref[...])
    x1_ref[...] = x1.astype(BF16)
    bw = 2 * FFN_COLS

    u_refs = slot_refs[0:FFN_SLOTS]
    h_refs = slot_refs[FFN_SLOTS:2 * FFN_SLOTS]

    def up(j):
        u_ref = u_refs[j % FFN_SLOTS]
        cols = slice(j * bw, (j + 1) * bw)
        gcols = slice(j * FFN_COLS, (j + 1) * FFN_COLS)
        vcols = slice(D_FF + j * FFN_COLS, D_FF + (j + 1) * FFN_COLS)
        u_ref[0:8, :] = carry_ref[:, cols]
        u_ref[8:8 + tm, 0:FFN_COLS] = jnp.dot(x1_ref[...], wup_ref[:, gcols], preferred_element_type=F32)
        u_ref[8:8 + tm, FFN_COLS:bw] = jnp.dot(x1_ref[...], wup_ref[:, vcols], preferred_element_type=F32)
        carry_ref[:, cols] = u_ref[tm:tm + 8, :]

    def gated(j):
        u_ref = u_refs[j % FFN_SLOTS]
        halves = []
        for part in range(2):
            cols = slice(part * D_FF + j * FFN_COLS, part * D_FF + (j + 1) * FFN_COLS)
            ucols = slice(part * FFN_COLS, (part + 1) * FFN_COLS)
            halves.append(cb_ref[:, cols] + cw_ref[0:1, cols] * u_ref[6:6 + tm, ucols]
                          + cw_ref[1:2, cols] * u_ref[7:7 + tm, ucols]
                          + cw_ref[2:3, cols] * u_ref[8:8 + tm, ucols])
        gate, val = halves
        h_refs[j % FFN_SLOTS][...] = (gate * _sigmoid(gate) * val).astype(BF16)

    def down(j):
        part = jnp.dot(h_refs[j % FFN_SLOTS][...], wdn_ref[j * FFN_COLS:(j + 1) * FFN_COLS, :],
                       preferred_element_type=F32)
        if j == 0:
            acc_ref[...] = part
        else:
            acc_ref[...] += part

    n_blocks = D_FF // FFN_COLS
    for j in range(min(FFN_LOOKAHEAD, n_blocks)):
        up(j)
    for j in range(n_blocks + 1):
        if j < n_blocks:
            if j + FFN_LOOKAHEAD < n_blocks:
                up(j + FFN_LOOKAHEAD)
            gated(j)
        if j >= 1:
            down(j - 1)
    o_ref[...] = _layer_norm(ALPHA * x1 + acc_ref[...], ln2g_ref[...], ln2b_ref[...])


def _out_ffn(y, x, wout, ln1g, ln1b, wup, cw, cb, wdn, ln2g, ln2b):
    batch, seq, _ = x.shape
    tm = min(ROW_TILE, seq)
    rows = lambda b, s: (b, s, 0)
    c2 = lambda b, s: (0, 0)
    one = pl.Buffered(1)
    return pl.pallas_call(
        _ffn_kernel,
        grid=(batch, seq // tm),
        in_specs=[
            pl.BlockSpec((None, tm, D_MODEL), rows),
            pl.BlockSpec((None, tm, D_MODEL), rows),
            pl.BlockSpec((D_MODEL, D_MODEL), c2, pipeline_mode=one),
            pl.BlockSpec((1, D_MODEL), c2),
            pl.BlockSpec((1, D_MODEL), c2),
            pl.BlockSpec((D_MODEL, 2 * D_FF), c2, pipeline_mode=one),
            pl.BlockSpec((CONV_W, 2 * D_FF), c2),
            pl.BlockSpec((1, 2 * D_FF), c2),
            pl.BlockSpec((D_FF, D_MODEL), c2, pipeline_mode=one),
            pl.BlockSpec((1, D_MODEL), c2),
            pl.BlockSpec((1, D_MODEL), c2),
        ],
        out_specs=pl.BlockSpec((None, tm, D_MODEL), rows),
        out_shape=jax.ShapeDtypeStruct((batch, seq, D_MODEL), F32),
        scratch_shapes=[pltpu.VMEM((8, 2 * D_FF), F32),
                        pltpu.VMEM((tm, D_MODEL), F32),
                        pltpu.VMEM((tm, D_MODEL), BF16),
                        *[pltpu.VMEM((tm + 8, 2 * FFN_COLS), F32) for _ in range(FFN_SLOTS)],
                        *[pltpu.VMEM((tm, FFN_COLS), BF16) for _ in range(FFN_SLOTS)]],
        compiler_params=pltpu.CompilerParams(dimension_semantics=("parallel", "arbitrary"),
                                             vmem_limit_bytes=VMEM_LIMIT),
        name="out_ffn",
    )(y, x, wout, ln1g, ln1b, wup, cw, cb, wdn, ln2g, ln2b)


def _hi_lo(w):
    hi = w.astype(BF16)
    return hi, (w - hi.astype(F32)).astype(BF16)


def _mixer_constants(seq):
    half = HEAD_DIM // 2
    lane = jnp.arange(PAIR)
    inv = ROPE_BASE ** (-jnp.arange(half, dtype=F32) / half)
    ang = jnp.arange(seq, dtype=F32)[:, None] * inv[None, :]
    idx = lane % half
    cos_t = jnp.cos(ang)[:, idx]
    sign = jnp.where((lane % HEAD_DIM) < half, -1.0, 1.0).astype(F32)
    sin_t = jnp.sin(ang)[:, idx] * sign[None, :]

    log_g = jnp.log(1.0 - jnp.exp2(-5.0 - jnp.arange(RET_HEADS, dtype=F32)))
    t = jnp.arange(CHUNK, dtype=F32)
    lane_s = (lane % CHUNK).astype(F32)
    lane_h = lane // CHUNK
    pairs = jnp.arange(RET_HEADS // 2)
    lg_cat = log_g[2 * pairs[:, None] + lane_h[None, :]]
    dist = jnp.abs(t[:, None] - lane_s[None, :])
    dmat = jnp.exp(lg_cat[:, None, :] * dist[None])
    lg_lane = jnp.repeat(log_g, HEAD_DIM)
    qdec = jnp.exp(lg_lane[None, :] * (t + 1.0)[:, None])
    kdec = jnp.exp(lg_lane[None, :] * (CHUNK - 1.0 - t)[:, None])
    chunk_decay = jnp.exp(lg_lane * CHUNK)
    return (cos_t, sin_t, dmat.astype(F32), qdec.astype(F32), kdec.astype(F32)), chunk_decay


def _attention_bias(rel_bias):
    t = jnp.arange(BAND + CHUNK - 1)
    ext_idx = jnp.clip(BAND_PAD + (CHUNK - 1) - t, -(CHUNK - 1), REL_CLIP) + (CHUNK - 1)
    ext = rel_bias.astype(F32)[:, ext_idx]
    bias = jnp.stack([ext[:, CHUNK - 1 - i:CHUNK - 1 - i + BAND] for i in range(CHUNK)], axis=1)
    return bias.reshape(ATT_HEADS // 2, PAIR, BAND)


def kernel(x, ln_in_g, ln_in_b, w_in, rw_mu, rw_w0, rw_w_up, rw_a0, rw_a_up, rw_g_up, rw_k_k, rw_k_a, rw_r_k, rw_ln_g, rw_ln_b, ret_gn_g, ret_gn_b, attn_rel_bias, w_out, ln1_g, ln1_b, ffn_w_up, ffn_conv_w, ffn_conv_b, ffn_w_down, ln2_g, ln2_b):
    batch, seq, _ = x.shape
    consts, chunk_decay = _mixer_constants(seq)
    row2 = lambda t: t.reshape(1, -1).astype(F32)
    xcur = x.reshape(batch * seq, D_MODEL)
    for l in range(DEPTH):
        if l == 0:
            z, xcur = _project(xcur, row2(ln_in_g), row2(ln_in_b), w_in[l].astype(BF16), True)
        else:
            (z,) = _project(xcur, row2(ln_in_g), row2(ln_in_b), w_in[l].astype(BF16), False)

        zero_row = jnp.zeros((RWKV_W,), F32)
        rwv = jnp.stack([rw_w0[l], rw_a0[l], rw_k_k[l], rw_k_a[l], rw_r_k[l].reshape(-1),
                         rw_ln_g[l], rw_ln_b[l], zero_row]).astype(F32)
        wwa = jnp.zeros((PAIR, 2 * RWKV_W), F32)
        wwa = wwa.at[0:DECAY_LORA, 0:RWKV_W].set(rw_w_up[l].astype(F32))
        wwa = wwa.at[DECAY_LORA:PAIR, RWKV_W:2 * RWKV_W].set(rw_a_up[l].astype(F32))
        wwa_hi, wwa_lo = _hi_lo(wwa)
        retv = jnp.concatenate([jnp.stack([ret_gn_g[l].astype(F32), ret_gn_b[l].astype(F32), chunk_decay]),
                                jnp.zeros((5, RET_W), F32)])
        mix_params = (row2(rw_mu[l]), rwv, wwa_hi, wwa_lo, rw_g_up[l].astype(BF16), retv,
                      _attention_bias(attn_rel_bias[l]))
        y = _mixers(z.reshape(batch, seq, N_IN_COLS), mix_params, consts, batch, seq)

        xcur = _out_ffn(y, xcur.reshape(batch, seq, D_MODEL), w_out[l].astype(BF16),
                        row2(ln1_g[l]), row2(ln1_b[l]), ffn_w_up[l].astype(BF16),
                        ffn_conv_w[l].astype(F32), row2(ffn_conv_b[l]), ffn_w_down[l].astype(BF16),
                        row2(ln2_g[l]), row2(ln2_b[l])).reshape(batch * seq, D_MODEL)
    return xcur.reshape(batch, seq, D_MODEL)
```

```python
import functools

import jax
import jax.numpy as jnp
from jax import lax
from jax.experimental import pallas as pl
from jax.experimental.pallas import tpu as pltpu

F32 = jnp.float32
BF16 = jnp.bfloat16

D_MODEL = 1024
DEPTH = 2
CHUNK = 64
HEAD_DIM = 64
PAIR = 2 * HEAD_DIM
RWKV_HEADS, RET_HEADS, ATT_HEADS = 6, 6, 4
RWKV_W, RET_W, ATT_W = 384, 384, 256
DECAY_LORA, AAA_LORA, GATE_LORA = 64, 64, 128
RWKV_COLS = 3 * RWKV_W + DECAY_LORA + AAA_LORA + GATE_LORA
RET_COLS = 4 * RET_W
ATT_COLS = 3 * ATT_W
N_IN_COLS = RWKV_COLS + RET_COLS + ATT_COLS
RET_OFF = RWKV_COLS
ATT_OFF = RWKV_COLS + RET_COLS
BAND_PREV_CHUNKS = 8
BAND_PAD = BAND_PREV_CHUNKS * CHUNK
BAND = BAND_PAD + CHUNK
REL_CLIP = 128
D_FF = 2816
CONV_W = 3
ALPHA = (2 * DEPTH) ** 0.25
ROPE_BASE = 10000.0
LN_EPS = 1e-5
RWKV_GN_EPS = 64e-5
RET_GN_EPS = 1e-5

ROW_TILE = 512
FFN_COLS = 256
FFN_LOOKAHEAD = 2
FFN_SLOTS = FFN_LOOKAHEAD + 1
MIX_BATCH = 4
PREP_FILL = 2
VMEM_LIMIT = 56 * 1024 * 1024

assert CHUNK == HEAD_DIM


def _dot(a, b):
    return jnp.dot(a.astype(BF16), b.astype(BF16), preferred_element_type=F32)


def _dot_nt(a, b):
    return lax.dot_general(a.astype(BF16), b.astype(BF16), (((1,), (1,)), ((), ())),
                           preferred_element_type=F32)


def _dot_tn(a, b):
    return lax.dot_general(a.astype(BF16), b.astype(BF16), (((0,), (0,)), ((), ())),
                           preferred_element_type=F32)


def _split2(x):
    hi = x.astype(BF16)
    lo = (x - hi.astype(F32)).astype(BF16)
    return hi, lo


def _split3(x):
    hi = x.astype(BF16)
    r1 = x - hi.astype(F32)
    mid = r1.astype(BF16)
    lo = (r1 - mid.astype(F32)).astype(BF16)
    return hi, mid, lo


def _dot_x3(x, w_hi, w_lo):
    x_hi, x_lo = _split2(x)
    return (jnp.dot(x_hi, w_hi, preferred_element_type=F32)
            + jnp.dot(x_hi, w_lo, preferred_element_type=F32)
            + jnp.dot(x_lo, w_hi, preferred_element_type=F32))


def _layer_norm(x, g, b, eps=LN_EPS):
    mu = jnp.mean(x, axis=-1, keepdims=True)
    d = x - mu
    var = jnp.mean(d * d, axis=-1, keepdims=True)
    return d * lax.rsqrt(var + eps) * g + b


def _sigmoid(x):
    return 1.0 / (1.0 + jnp.exp(-x))


def _proj_kernel(apply_ln, x_ref, g_ref, b_ref, w_ref, z_ref, *xn_ref):
    x = x_ref[...]
    if apply_ln:
        x = _layer_norm(x, g_ref[...], b_ref[...])
        xn_ref[0][...] = x
    z_ref[...] = jnp.dot(x.astype(BF16), w_ref[...], preferred_element_type=F32).astype(z_ref.dtype)


def _project(x2d, ln_g, ln_b, w_bf16, apply_ln):
    m = x2d.shape[0]
    tm = ROW_TILE
    row = lambda i: (i, 0)
    const = lambda i: (0, 0)
    out_shape = [jax.ShapeDtypeStruct((m, N_IN_COLS), BF16)]
    out_specs = [pl.BlockSpec((tm, N_IN_COLS), row)]
    if apply_ln:
        out_shape.append(jax.ShapeDtypeStruct((m, D_MODEL), F32))
        out_specs.append(pl.BlockSpec((tm, D_MODEL), row))
    outs = pl.pallas_call(
        functools.partial(_proj_kernel, apply_ln),
        grid=(m // tm,),
        in_specs=[pl.BlockSpec((tm, D_MODEL), row),
                  pl.BlockSpec((1, D_MODEL), const),
                  pl.BlockSpec((1, D_MODEL), const),
                  pl.BlockSpec((D_MODEL, N_IN_COLS), const, pipeline_mode=pl.Buffered(1))],
        out_specs=out_specs,
        out_shape=out_shape,
        compiler_params=pltpu.CompilerParams(dimension_semantics=("parallel",),
                                             vmem_limit_bytes=VMEM_LIMIT),
        name="in_proj",
    )(x2d, ln_g, ln_b, w_bf16)
    return outs


def _mixer_kernel(z_ref, mu_ref, rwv_ref, wwa_hi_ref, wwa_lo_ref, gup_ref,
                  retv_ref, cos_ref, sin_ref, dmat_ref, qdec_ref, kdec_ref, bias_ref,
                  y_ref, s_ref, r_ref, carry_ref, kbuf_ref, vbuf_ref):
    n = pl.program_id(1)
    C = CHUNK

    @pl.when(n == 0)
    def _():
        s_ref[...] = jnp.zeros_like(s_ref)
        r_ref[...] = jnp.zeros_like(r_ref)
        carry_ref[...] = jnp.zeros_like(carry_ref)
        kbuf_ref[:, :, 0:BAND_PAD, :] = jnp.zeros((MIX_BATCH, ATT_HEADS // 2, BAND_PAD, PAIR), BF16)
        vbuf_ref[:, :, 0:BAND_PAD, :] = jnp.zeros((MIX_BATCH, ATT_HEADS // 2, BAND_PAD, PAIR), BF16)

    lane = lax.broadcasted_iota(jnp.int32, (C, PAIR), 1)
    time = lax.broadcasted_iota(jnp.int32, (C, PAIR), 0)
    head0 = lane < HEAD_DIM
    cat_incl = time >= (lane % C)
    cat_strict = time > (lane % C)

    def blk(x):
        zero = jnp.zeros_like(x)
        return jnp.concatenate([jnp.where(head0, x, zero), jnp.where(head0, zero, x)], axis=0)

    def zcols(i, start, width):
        return z_ref[i, :, start:start + width].astype(F32)

    r2 = lax.broadcasted_iota(jnp.int32, (2 * PAIR, 2 * PAIR), 0)
    c2 = lax.broadcasted_iota(jnp.int32, (2 * PAIR, 2 * PAIR), 1)
    head_ones = ((r2 // HEAD_DIM) == (c2 // HEAD_DIM)).astype(BF16)

    def head_sums(slabs):
        x = jnp.concatenate([jnp.concatenate(slabs[i:i + 2], axis=1) for i in range(0, len(slabs), 2)], axis=0)
        s = jnp.dot(x.astype(BF16), head_ones, preferred_element_type=F32)
        return [s[(i // 2) * C:(i // 2 + 1) * C, (i % 2) * PAIR:(i % 2 + 1) * PAIR] for i in range(len(slabs))]

    def head_norms(ys, gs, bs, epss):
        mus = head_sums(ys)
        ds = [y - mu * (1.0 / HEAD_DIM) for y, mu in zip(ys, mus)]
        vs = head_sums([d * d for d in ds])
        return [d * lax.rsqrt(var * (1.0 / HEAD_DIM) + eps) * g + b
                for d, var, g, b, eps in zip(ds, vs, gs, bs, epss)]

    w0 = rwv_ref[0:1, :]
    a0 = rwv_ref[1:2, :]
    k_k = rwv_ref[2:3, :]
    k_a = rwv_ref[3:4, :]
    r_k = rwv_ref[4:5, :]
    ln_g = rwv_ref[5:6, :]
    ln_b = rwv_ref[6:7, :]
    gn_g = retv_ref[0:1, :]
    gn_b = retv_ref[1:2, :]
    chunk_decay = retv_ref[2:3, :]
    cos_t = cos_ref[...]
    sin_t = sin_ref[...]
    first_half = (lane % HEAD_DIM) < (HEAD_DIM // 2)
    row = lax.broadcasted_iota(jnp.int32, (C, RWKV_COLS), 0)
    tri = (lax.broadcasted_iota(jnp.int32, (C, C), 0)
           >= lax.broadcasted_iota(jnp.int32, (C, C), 1)).astype(BF16)
    off = pl.multiple_of(n * C, C)
    col = lax.broadcasted_iota(jnp.int32, (PAIR, BAND), 1)
    valid = col >= BAND_PAD - n * C
    pairs = [slice(p * PAIR, (p + 1) * PAIR) for p in range(RWKV_HEADS // 2)]

    def rope(t):
        partner = jnp.where(first_half, pltpu.roll(t, PAIR - HEAD_DIM // 2, 1),
                            pltpu.roll(t, HEAD_DIM // 2, 1))
        return t * cos_t + partner * sin_t

    def prepare(i, sink):
        zc = zcols(i, 0, RWKV_COLS)
        prev = jnp.where(row == 0, carry_ref[i, 7:8, :], pltpu.roll(zc, 1, 0))
        carry_ref[i] = zc[C - 8:C, :]
        zs = zc + (prev - zc) * mu_ref[...]
        r = zs[:, 0:RWKV_W]
        k = zs[:, RWKV_W:2 * RWKV_W]
        v = zs[:, 2 * RWKV_W:3 * RWKV_W]
        wa_l = zs[:, 3 * RWKV_W:3 * RWKV_W + PAIR]
        g_l = zs[:, 3 * RWKV_W + PAIR:RWKV_COLS]

        wa_in = jnp.where(head0, jnp.tanh(wa_l), wa_l)
        lora = _dot_x3(wa_in, wwa_hi_ref[...], wwa_lo_ref[...])
        w_pre = w0 + lora[:, 0:RWKV_W]
        a = _sigmoid(a0 + lora[:, RWKV_W:2 * RWKV_W])
        g = jnp.dot(_sigmoid(g_l).astype(BF16), gup_ref[...], preferred_element_type=F32)
        sp = jnp.maximum(-w_pre, 0.0) + jnp.log(1.0 + jnp.exp(-jnp.abs(w_pre)))
        log_decay = -jnp.exp(-sp - 0.5)
        yield

        ld3 = jnp.concatenate(_split3(log_decay), axis=1)
        cs = jnp.dot(tri, ld3, preferred_element_type=F32)
        cum = cs[:, 0:RWKV_W] + cs[:, RWKV_W:2 * RWKV_W] + cs[:, 2 * RWKV_W:3 * RWKV_W]
        e_incl = jnp.exp(cum)
        e_excl = jnp.exp(cum - log_decay)
        e_inv = jnp.exp(-cum)
        yield

        kk = k * k_k
        k2 = k * (1.0 + (a - 1.0) * k_a)
        rkr = r * k2 * r_k
        sums = head_sums([kk[:, sl] * kk[:, sl] for sl in pairs] + [rkr[:, sl] for sl in pairs])
        kk_ss, rkr_sum = sums[0:3], sums[3:6]
        sink["vals"] = dict(v=v, g=g, rkr_sum=rkr_sum)
        sink["chains"] = chains = []
        yield

        for p, sl in enumerate(pairs):
            kk_p = kk[:, sl] / jnp.maximum(jnp.sqrt(kk_ss[p]), 1e-12)
            a_hat = -kk_p * e_excl[:, sl]
            r_hat = r[:, sl] * e_incl[:, sl]
            b_til = blk(kk_p * a[:, sl] * e_inv[:, sl])
            k_til = blk(k2[:, sl] * e_inv[:, sl])
            chains.append(dict(
                i=i, p=p,
                ar=jnp.concatenate([a_hat, r_hat], axis=0).astype(BF16),
                bk=jnp.concatenate([b_til, k_til], axis=0).astype(BF16),
                v_b=blk(v[:, sl]).astype(BF16),
                p_end=e_incl[C - 1:C, sl]))
            yield

    y_ret = {}
    gates = {}
    y_att = {}

    def retention_tasks(i, p):
        st = {}
        sl = pairs[p]
        base = RET_OFF + p * PAIR

        def scores():
            q = rope(zcols(i, base, PAIR))
            kr = rope(zcols(i, base + RET_W, PAIR)) * (HEAD_DIM ** -0.5)
            gg = zcols(i, base + 3 * RET_W, PAIR)
            gates[i, p] = gg * _sigmoid(gg)
            st["q_dec"] = q * qdec_ref[:, sl]
            st["kd_b"] = blk(kr * kdec_ref[:, sl])
            st["v_b"] = blk(z_ref[i, :, base + 2 * RET_W:base + 2 * RET_W + PAIR].astype(BF16))
            st["scores"] = _dot_nt(q, blk(kr)) * dmat_ref[p]

        def output():
            r_old = r_ref[i, p]
            lhs = jnp.concatenate([st["scores"], st["q_dec"]], axis=1)
            rhs = jnp.concatenate([st["v_b"], r_old.astype(BF16)], axis=0)
            y_ret[i, p] = _dot(lhs, rhs)
            r_ref[i, p] = r_old * chunk_decay[:, sl] + _dot_tn(st["kd_b"], st["v_b"])

        return [scores, output]

    def attention_tasks(i, p):
        st = {}
        base = ATT_OFF + p * PAIR

        def scores():
            q = zcols(i, base, PAIR) * (HEAD_DIM ** -0.5)
            kbuf_ref[i, p, pl.ds(BAND_PAD + off, C), :] = (
                z_ref[i, :, base + ATT_W:base + ATT_W + PAIR].astype(BF16))
            vbuf_ref[i, p, pl.ds(BAND_PAD + off, C), :] = (
                z_ref[i, :, base + 2 * ATT_W:base + 2 * ATT_W + PAIR].astype(BF16))
            k_band = kbuf_ref[i, p, pl.ds(off, BAND), :]
            s = _dot_nt(blk(q), k_band) + bias_ref[p]
            s = jnp.where(valid, s, -1e30)
            m = jnp.max(s, axis=-1, keepdims=True)
            st["e"] = jnp.exp(s - m)
            st["l"] = jnp.sum(st["e"], axis=-1, keepdims=True)

        def output():
            v_band = vbuf_ref[i, p, pl.ds(off, BAND), :]
            o = _dot(st["e"], v_band) / st["l"]
            y_att[i, p] = jnp.where(head0, o[0:C], o[C:2 * C])

        return [scores, output]

    y_rwkv = {}

    def chain_stages(chains):
        for c in chains:
            c["s_old"] = s_ref[c["i"], c["p"]]
            big = _dot_nt(c["ar"], c["bk"])
            c["pw"] = jnp.where(cat_strict, big[0:C, 0:PAIR], 0.0).astype(BF16)
            c["a_ak"] = jnp.where(cat_strict, big[0:C, PAIR:2 * PAIR], 0.0)
            c["l_rbk"] = jnp.concatenate(
                [jnp.where(cat_incl, big[C:2 * C, 0:PAIR], 0.0),
                 jnp.where(cat_incl, big[C:2 * C, PAIR:2 * PAIR], 0.0)], axis=1).astype(BF16)
        yield
        for c in chains:
            c["ars"] = _dot_nt(c["ar"], c["s_old"])
        yield
        for c in chains:
            c["x"] = c["ars"][0:C] + _dot(c["a_ak"], c["v_b"])
        yield
        for _ in range(5):
            for c in chains:
                rhs = jnp.concatenate([blk(c["x"]).astype(BF16), blk(c["pw"])], axis=1)
                res = jnp.dot(c["pw"], rhs, preferred_element_type=F32)
                c["x"] = c["x"] + res[:, 0:PAIR]
                c["pw"] = res[:, PAIR:2 * PAIR].astype(BF16)
            yield
        for c in chains:
            c["x"] = c["x"] + jnp.dot(c["pw"], blk(c["x"]).astype(BF16), preferred_element_type=F32)
            c["uv"] = jnp.concatenate([blk(c["x"]).astype(BF16), c["v_b"]], axis=0)
        yield
        for c in chains:
            y_rwkv[c["i"], c["p"]] = c["ars"][C:2 * C] + jnp.dot(c["l_rbk"], c["uv"], preferred_element_type=F32)
        yield
        for c in chains:
            s_ref[c["i"], c["p"]] = (c["s_old"] + _dot_tn(c["uv"], c["bk"])) * c["p_end"]
        yield

    seqs = range(MIX_BATCH)
    first, second = [], []
    for i in seqs:
        for p in range(RET_HEADS // 2):
            t = retention_tasks(i, p)
            first.append(t[0])
            second.append(t[1])
        for p in range(ATT_HEADS // 2):
            t = attention_tasks(i, p)
            first.append(t[0])
            second.append(t[1])
    fillers = first + second

    def fill(count):
        for _ in range(count):
            if fillers:
                fillers.pop(0)()

    sinks = {i: {} for i in seqs}
    for i in seqs:
        for _ in prepare(i, sinks[i]):
            fill(PREP_FILL)
    for _ in chain_stages([c for i in seqs for c in sinks[i]["chains"]]):
        fill(1)
    fill(len(fillers))
    seq_vals = {i: sinks[i]["vals"] for i in seqs}

    normed_all = head_norms(
        [y for i in seqs for y in ([y_rwkv[i, p] for p in range(3)] + [y_ret[i, p] for p in range(3)])],
        ([ln_g[:, sl] for sl in pairs] + [gn_g[:, sl] for sl in pairs]) * MIX_BATCH,
        ([ln_b[:, sl] for sl in pairs] + [gn_b[:, sl] for sl in pairs]) * MIX_BATCH,
        ([RWKV_GN_EPS] * 3 + [RET_GN_EPS] * 3) * MIX_BATCH)
    for i in seqs:
        vals = seq_vals[i]
        normed = normed_all[6 * i:6 * i + 6]
        out = []
        for p, sl in enumerate(pairs):
            out.append((normed[p] + vals["rkr_sum"][p] * vals["v"][:, sl]) * vals["g"][:, sl])
        for p in range(RET_HEADS // 2):
            out.append(normed[3 + p] * gates[i, p])
        for p in range(ATT_HEADS // 2):
            out.append(y_att[i, p])
        y_ref[i] = jnp.concatenate(out, axis=1).astype(y_ref.dtype)


def _mixers(z, mix_params, consts, batch, seq):
    nc = seq // CHUNK
    nb = MIX_BATCH
    c2 = lambda b, n: (0, 0)
    c3 = lambda b, n: (0, 0, 0)
    (mu, rwv, wwa_hi, wwa_lo, gup, retv, bias) = mix_params
    (cos_t, sin_t, dmat, qdec, kdec) = consts
    in_specs = [
        pl.BlockSpec((nb, CHUNK, N_IN_COLS), lambda b, n: (b, n, 0)),
        pl.BlockSpec((1, RWKV_COLS), c2),
        pl.BlockSpec((8, RWKV_W), c2),
        pl.BlockSpec((PAIR, 2 * RWKV_W), c2),
        pl.BlockSpec((PAIR, 2 * RWKV_W), c2),
        pl.BlockSpec((GATE_LORA, RWKV_W), c2),
        pl.BlockSpec((8, RET_W), c2),
        pl.BlockSpec((CHUNK, PAIR), lambda b, n: (n, 0)),
        pl.BlockSpec((CHUNK, PAIR), lambda b, n: (n, 0)),
        pl.BlockSpec((RET_HEADS // 2, CHUNK, PAIR), c3),
        pl.BlockSpec((CHUNK, RET_W), c2),
        pl.BlockSpec((CHUNK, RET_W), c2),
        pl.BlockSpec((ATT_HEADS // 2, PAIR, BAND), c3),
    ]
    return pl.pallas_call(
        _mixer_kernel,
        grid=(batch // nb, nc),
        in_specs=in_specs,
        out_specs=pl.BlockSpec((nb, CHUNK, D_MODEL), lambda b, n: (b, n, 0)),
        out_shape=jax.ShapeDtypeStruct((batch, seq, D_MODEL), BF16),
        scratch_shapes=[
            pltpu.VMEM((nb, RWKV_HEADS // 2, PAIR, PAIR), F32),
            pltpu.VMEM((nb, RET_HEADS // 2, PAIR, PAIR), F32),
            pltpu.VMEM((nb, 8, RWKV_COLS), F32),
            pltpu.VMEM((nb, ATT_HEADS // 2, BAND_PAD + seq, PAIR), BF16),
            pltpu.VMEM((nb, ATT_HEADS // 2, BAND_PAD + seq, PAIR), BF16),
        ],
        compiler_params=pltpu.CompilerParams(dimension_semantics=("parallel", "arbitrary"),
                                             vmem_limit_bytes=VMEM_LIMIT),
        name="mixers",
    )(z, mu, rwv, wwa_hi, wwa_lo, gup, retv, cos_t, sin_t, dmat, qdec, kdec, bias)


def _ffn_kernel(y_ref, x_ref, wout_ref, ln1g_ref, ln1b_ref, wup_ref, cw_ref, cb_ref, wdn_ref,
                ln2g_ref, ln2b_ref, o_ref, carry_ref, acc_ref, x1_ref, *slot_refs):
    tm = x_ref.shape[0]

    @pl.when(pl.program_id(1) == 0)
    def _():
        carry_ref[...] = jnp.zeros_like(carry_ref)

    h = ALPHA * x_ref[...] + jnp.dot(y_ref[...], wout_ref[...], preferred_element_type=F32)
    x1 = _layer_norm(h, ln1g_ref[...], ln1b_ref[...])
    x1_ref[...] = x1.astype(BF16)
    bw = 2 * FFN_COLS

    u_refs = slot_refs[0:FFN_SLOTS]
    h_refs = slot_refs[FFN_SLOTS:2 * FFN_SLOTS]

    def up(j):
        u_ref = u_refs[j % FFN_SLOTS]
        cols = slice(j * bw, (j + 1) * bw)
        gcols = slice(j * FFN_COLS, (j + 1) * FFN_COLS)
        vcols = slice(D_FF + j * FFN_COLS, D_FF + (j + 1) * FFN_COLS)
        u_ref[0:8, :] = carry_ref[:, cols]
        u_ref[8:8 + tm, 0:FFN_COLS] = jnp.dot(x1_ref[...], wup_ref[:, gcols], preferred_element_type=F32)
        u_ref[8:8 + tm, FFN_COLS:bw] = jnp.dot(x1_ref[...], wup_ref[:, vcols], preferred_element_type=F32)
        carry_ref[:, cols] = u_ref[tm:tm + 8, :]

    def gated(j):
        u_ref = u_refs[j % FFN_SLOTS]
        halves = []
        for part in range(2):
            cols = slice(part * D_FF + j * FFN_COLS, part * D_FF + (j + 1) * FFN_COLS)
            ucols = slice(part * FFN_COLS, (part + 1) * FFN_COLS)
            halves.append(cb_ref[:, cols] + cw_ref[0:1, cols] * u_ref[6:6 + tm, ucols]
                          + cw_ref[1:2, cols] * u_ref[7:7 + tm, ucols]
                          + cw_ref[2:3, cols] * u_ref[8:8 + tm, ucols])
        gate, val = halves
        h_refs[j % FFN_SLOTS][...] = (gate * _sigmoid(gate) * val).astype(BF16)

    def down(j):
        part = jnp.dot(h_refs[j % FFN_SLOTS][...], wdn_ref[j * FFN_COLS:(j + 1) * FFN_COLS, :],
                       preferred_element_type=F32)
        if j == 0:
            acc_ref[...] = part
        else:
            acc_ref[...] += part

    n_blocks = D_FF // FFN_COLS
    for j in range(min(FFN_LOOKAHEAD, n_blocks)):
        up(j)
    for j in range(n_blocks + 1):
        if j < n_blocks:
            if j + FFN_LOOKAHEAD < n_blocks:
                up(j + FFN_LOOKAHEAD)
            gated(j)
        if j >= 1:
            down(j - 1)
    o_ref[...] = _layer_norm(ALPHA * x1 + acc_ref[...], ln2g_ref[...], ln2b_ref[...])


def _out_ffn(y, x, wout, ln1g, ln1b, wup, cw, cb, wdn, ln2g, ln2b):
    batch, seq, _ = x.shape
    tm = min(ROW_TILE, seq)
    rows = lambda b, s: (b, s, 0)
    c2 = lambda b, s: (0, 0)
    one = pl.Buffered(1)
    return pl.pallas_call(
        _ffn_kernel,
        grid=(batch, seq // tm),
        in_specs=[
            pl.BlockSpec((None, tm, D_MODEL), rows),
            pl.BlockSpec((None, tm, D_MODEL), rows),
            pl.BlockSpec((D_MODEL, D_MODEL), c2, pipeline_mode=one),
            pl.BlockSpec((1, D_MODEL), c2),
            pl.BlockSpec((1, D_MODEL), c2),
            pl.BlockSpec((D_MODEL, 2 * D_FF), c2, pipeline_mode=one),
            pl.BlockSpec((CONV_W, 2 * D_FF), c2),
            pl.BlockSpec((1, 2 * D_FF), c2),
            pl.BlockSpec((D_FF, D_MODEL), c2, pipeline_mode=one),
            pl.BlockSpec((1, D_MODEL), c2),
            pl.BlockSpec((1, D_MODEL), c2),
        ],
        out_specs=pl.BlockSpec((None, tm, D_MODEL), rows),
        out_shape=jax.ShapeDtypeStruct((batch, seq, D_MODEL), F32),
        scratch_shapes=[pltpu.VMEM((8, 2 * D_FF), F32),
                        pltpu.VMEM((tm, D_MODEL), F32),
                        pltpu.VMEM((tm, D_MODEL), BF16),
                        *[pltpu.VMEM((tm + 8, 2 * FFN_COLS), F32) for _ in range(FFN_SLOTS)],
                        *[pltpu.VMEM((tm, FFN_COLS), BF16) for _ in range(FFN_SLOTS)]],
        compiler_params=pltpu.CompilerParams(dimension_semantics=("parallel", "arbitrary"),
                                             vmem_limit_bytes=VMEM_LIMIT),
        name="out_ffn",
    )(y, x, wout, ln1g, ln1b, wup, cw, cb, wdn, ln2g, ln2b)


def _hi_lo(w):
    hi = w.astype(BF16)
    return hi, (w - hi.astype(F32)).astype(BF16)


def _mixer_constants(seq):
    half = HEAD_DIM // 2
    lane = jnp.arange(PAIR)
    inv = ROPE_BASE ** (-jnp.arange(half, dtype=F32) / half)
    ang = jnp.arange(seq, dtype=F32)[:, None] * inv[None, :]
    idx = lane % half
    cos_t = jnp.cos(ang)[:, idx]
    sign = jnp.where((lane % HEAD_DIM) < half, -1.0, 1.0).astype(F32)
    sin_t = jnp.sin(ang)[:, idx] * sign[None, :]

    log_g = jnp.log(1.0 - jnp.exp2(-5.0 - jnp.arange(RET_HEADS, dtype=F32)))
    t = jnp.arange(CHUNK, dtype=F32)
    lane_s = (lane % CHUNK).astype(F32)
    lane_h = lane // CHUNK
    pairs = jnp.arange(RET_HEADS // 2)
    lg_cat = log_g[2 * pairs[:, None] + lane_h[None, :]]
    dist = jnp.abs(t[:, None] - lane_s[None, :])
    dmat = jnp.exp(lg_cat[:, None, :] * dist[None])
    lg_lane = jnp.repeat(log_g, HEAD_DIM)
    qdec = jnp.exp(lg_lane[None, :] * (t + 1.0)[:, None])
    kdec = jnp.exp(lg_lane[None, :] * (CHUNK - 1.0 - t)[:, None])
    chunk_decay = jnp.exp(lg_lane * CHUNK)
    return (cos_t, sin_t, dmat.astype(F32), qdec.astype(F32), kdec.astype(F32)), chunk_decay


def _attention_bias(rel_bias):
    t = jnp.arange(BAND + CHUNK - 1)
    ext_idx = jnp.clip(BAND_PAD + (CHUNK - 1) - t, -(CHUNK - 1), REL_CLIP) + (CHUNK - 1)
    ext = rel_bias.astype(F32)[:, ext_idx]
    bias = jnp.stack([ext[:, CHUNK - 1 - i:CHUNK - 1 - i + BAND] for i in range(CHUNK)], axis=1)
    return bias.reshape(ATT_HEADS // 2, PAIR, BAND)


def kernel(x, ln_in_g, ln_in_b, w_in, rw_mu, rw_w0, rw_w_up, rw_a0, rw_a_up, rw_g_up, rw_k_k, rw_k_a, rw_r_k, rw_ln_g, rw_ln_b, ret_gn_g, ret_gn_b, attn_rel_bias, w_out, ln1_g, ln1_b, ffn_w_up, ffn_conv_w, ffn_conv_b, ffn_w_down, ln2_g, ln2_b):
    batch, seq, _ = x.shape
    consts, chunk_decay = _mixer_constants(seq)
    row2 = lambda t: t.reshape(1, -1).astype(F32)
    xcur = x.reshape(batch * seq, D_MODEL)
    for l in range(DEPTH):
        if l == 0:
            z, xcur = _project(xcur, row2(ln_in_g), row2(ln_in_b), w_in[l].astype(BF16), True)
        else:
            (z,) = _project(xcur, row2(ln_in_g), row2(ln_in_b), w_in[l].astype(BF16), False)

        zero_row = jnp.zeros((RWKV_W,), F32)
        rwv = jnp.stack([rw_w0[l], rw_a0[l], rw_k_k[l], rw_k_a[l], rw_r_k[l].reshape(-1),
                         rw_ln_g[l], rw_ln_b[l], zero_row]).astype(F32)
        wwa = jnp.zeros((PAIR, 2 * RWKV_W), F32)
        wwa = wwa.at[0:DECAY_LORA, 0:RWKV_W].set(rw_w_up[l].astype(F32))
        wwa = wwa.at[DECAY_LORA:PAIR, RWKV_W:2 * RWKV_W].set(rw_a_up[l].astype(F32))
        wwa_hi, wwa_lo = _hi_lo(wwa)
        retv = jnp.concatenate([jnp.stack([ret_gn_g[l].astype(F32), ret_gn_b[l].astype(F32), chunk_decay]),
                                jnp.zeros((5, RET_W), F32)])
        mix_params = (row2(rw_mu[l]), rwv, wwa_hi, wwa_lo, rw_g_up[l].astype(BF16), retv,
                      _attention_bias(attn_rel_bias[l]))
        y = _mixers(z.reshape(batch, seq, N_IN_COLS), mix_params, consts, batch, seq)

        xcur = _out_ffn(y, xcur.reshape(batch, seq, D_MODEL), w_out[l].astype(BF16),
                        row2(ln1_g[l]), row2(ln1_b[l]), ffn_w_up[l].astype(BF16),
                        ffn_conv_w[l].astype(F32), row2(ffn_conv_b[l]), ffn_w_down[l].astype(BF16),
                        row2(ln2_g[l]), row2(ln2_b[l])).reshape(batch * seq, D_MODEL)
    return xcur.reshape(batch, seq, D_MODEL)
```

```python
import functools

import jax
import jax.numpy as jnp
from jax import lax
from jax.experimental import pallas as pl
from jax.experimental.pallas import tpu as pltpu

F32 = jnp.float32
BF16 = jnp.bfloat16

D_MODEL = 1024
DEPTH = 2
CHUNK = 64
HEAD_DIM = 64
PAIR = 2 * HEAD_DIM
RWKV_HEADS, RET_HEADS, ATT_HEADS = 6, 6, 4
RWKV_W, RET_W, ATT_W = 384, 384, 256
DECAY_LORA, AAA_LORA, GATE_LORA = 64, 64, 128
RWKV_COLS = 3 * RWKV_W + DECAY_LORA + AAA_LORA + GATE_LORA
RET_COLS = 4 * RET_W
ATT_COLS = 3 * ATT_W
N_IN_COLS = RWKV_COLS + RET_COLS + ATT_COLS
RET_OFF = RWKV_COLS
ATT_OFF = RWKV_COLS + RET_COLS
BAND_PREV_CHUNKS = 8
BAND_PAD = BAND_PREV_CHUNKS * CHUNK
BAND = BAND_PAD + CHUNK
REL_CLIP = 128
D_FF = 2816
CONV_W = 3
ALPHA = (2 * DEPTH) ** 0.25
ROPE_BASE = 10000.0
LN_EPS = 1e-5
RWKV_GN_EPS = 64e-5
RET_GN_EPS = 1e-5
DECAY_SCALE = -0.6065306597126334

ROW_TILE = 512
FFN_COLS = 256
FFN_SUB = 512
FFN_LOOKAHEAD = 2
FFN_SLOTS = FFN_LOOKAHEAD + 1
MIX_BATCH = 4
PREP_FILL = 1
ATT_LAG = 2
VMEM_LIMIT = 56 * 1024 * 1024

assert CHUNK == HEAD_DIM


def _dot(a, b):
    return jnp.dot(a.astype(BF16), b.astype(BF16), preferred_element_type=F32)


def _dot_nt(a, b):
    return lax.dot_general(a.astype(BF16), b.astype(BF16), (((1,), (1,)), ((), ())),
                           preferred_element_type=F32)


def _dot_tn(a, b):
    return lax.dot_general(a.astype(BF16), b.astype(BF16), (((0,), (0,)), ((), ())),
                           preferred_element_type=F32)


def _split2(x):
    hi = x.astype(BF16)
    lo = (x - hi.astype(F32)).astype(BF16)
    return hi, lo


def _split3(x):
    hi = x.astype(BF16)
    r1 = x - hi.astype(F32)
    mid = r1.astype(BF16)
    lo = (r1 - mid.astype(F32)).astype(BF16)
    return hi, mid, lo


def _dot_x3(x, w_hi, w_lo):
    x_hi, x_lo = _split2(x)
    return (jnp.dot(x_hi, w_hi, preferred_element_type=F32)
            + jnp.dot(x_hi, w_lo, preferred_element_type=F32)
            + jnp.dot(x_lo, w_hi, preferred_element_type=F32))


def _layer_norm(x, g, b, eps=LN_EPS):
    mu = jnp.mean(x, axis=-1, keepdims=True)
    d = x - mu
    var = jnp.mean(d * d, axis=-1, keepdims=True)
    return d * lax.rsqrt(var + eps) * g + b


def _sigmoid(x):
    return 1.0 / (1.0 + jnp.exp(-x))


def _proj_kernel(apply_ln, x_ref, g_ref, b_ref, w_ref, z_ref, *xn_ref):
    x = x_ref[...]
    if apply_ln:
        x = _layer_norm(x, g_ref[...], b_ref[...])
        xn_ref[0][...] = x
    z_ref[...] = jnp.dot(x.astype(BF16), w_ref[...], preferred_element_type=F32)


def _project(x2d, ln_g, ln_b, w_bf16, apply_ln):
    m = x2d.shape[0]
    tm = ROW_TILE
    row = lambda i: (i, 0)
    const = lambda i: (0, 0)
    out_shape = [jax.ShapeDtypeStruct((m, N_IN_COLS), F32)]
    out_specs = [pl.BlockSpec((tm, N_IN_COLS), row)]
    if apply_ln:
        out_shape.append(jax.ShapeDtypeStruct((m, D_MODEL), F32))
        out_specs.append(pl.BlockSpec((tm, D_MODEL), row))
    outs = pl.pallas_call(
        functools.partial(_proj_kernel, apply_ln),
        grid=(m // tm,),
        in_specs=[pl.BlockSpec((tm, D_MODEL), row),
                  pl.BlockSpec((1, D_MODEL), const),
                  pl.BlockSpec((1, D_MODEL), const),
                  pl.BlockSpec((D_MODEL, N_IN_COLS), const, pipeline_mode=pl.Buffered(1))],
        out_specs=out_specs,
        out_shape=out_shape,
        compiler_params=pltpu.CompilerParams(dimension_semantics=("parallel",),
                                             vmem_limit_bytes=VMEM_LIMIT),
        name="in_proj",
    )(x2d, ln_g, ln_b, w_bf16)
    return outs


def _mixer_kernel(z_ref, mu_ref, rwv_ref, wwa_hi_ref, wwa_lo_ref, gup_ref,
                  retv_ref, cos_ref, sin_ref, dmat_ref, qdec_ref, kdec_ref, bias_ref,
                  y_ref, s_ref, r_ref, carry_ref, kbuf_ref, vbuf_ref):
    n = pl.program_id(1)
    C = CHUNK

    @pl.when(n == 0)
    def _():
        s_ref[...] = jnp.zeros_like(s_ref)
        r_ref[...] = jnp.zeros_like(r_ref)
        carry_ref[...] = jnp.zeros_like(carry_ref)
        kbuf_ref[:, :, 0:BAND_PAD, :] = jnp.zeros((MIX_BATCH, ATT_HEADS // 2, BAND_PAD, PAIR), BF16)
        vbuf_ref[:, :, 0:BAND_PAD, :] = jnp.zeros((MIX_BATCH, ATT_HEADS // 2, BAND_PAD, PAIR), BF16)

    lane = lax.broadcasted_iota(jnp.int32, (C, PAIR), 1)
    time = lax.broadcasted_iota(jnp.int32, (C, PAIR), 0)
    head0 = lane < HEAD_DIM
    cat_incl = time >= (lane % C)
    cat_strict = time > (lane % C)

    def blk(x):
        zero = jnp.zeros_like(x)
        return jnp.concatenate([jnp.where(head0, x, zero), jnp.where(head0, zero, x)], axis=0)

    def zcols(i, start, width):
        return z_ref[i, :, start:start + width]

    r2 = lax.broadcasted_iota(jnp.int32, (2 * PAIR, 2 * PAIR), 0)
    c2 = lax.broadcasted_iota(jnp.int32, (2 * PAIR, 2 * PAIR), 1)
    head_ones = ((r2 // HEAD_DIM) == (c2 // HEAD_DIM)).astype(BF16)

    def head_sums(slabs):
        x = jnp.concatenate([jnp.concatenate(slabs[i:i + 2], axis=1) for i in range(0, len(slabs), 2)], axis=0)
        s = jnp.dot(x.astype(BF16), head_ones, preferred_element_type=F32)
        return [s[(i // 2) * C:(i // 2 + 1) * C, (i % 2) * PAIR:(i % 2 + 1) * PAIR] for i in range(len(slabs))]

    def head_norms(ys, gs, bs, epss):
        mus = head_sums(ys)
        ds = [y - mu * (1.0 / HEAD_DIM) for y, mu in zip(ys, mus)]
        vs = head_sums([d * d for d in ds])
        return [d * lax.rsqrt(var * (1.0 / HEAD_DIM) + eps) * g + b
                for d, var, g, b, eps in zip(ds, vs, gs, bs, epss)]

    w0 = rwv_ref[0:1, :]
    a0 = rwv_ref[1:2, :]
    k_k = rwv_ref[2:3, :]
    k_a = rwv_ref[3:4, :]
    r_k = rwv_ref[4:5, :]
    ln_g = rwv_ref[5:6, :]
    ln_b = rwv_ref[6:7, :]
    gn_g = retv_ref[0:1, :]
    gn_b = retv_ref[1:2, :]
    chunk_decay = retv_ref[2:3, :]
    cos_t = cos_ref[...]
    sin_t = sin_ref[...]
    first_half = (lane % HEAD_DIM) < (HEAD_DIM // 2)
    row = lax.broadcasted_iota(jnp.int32, (C, RWKV_COLS), 0)
    tri = (lax.broadcasted_iota(jnp.int32, (C, C), 0)
           >= lax.broadcasted_iota(jnp.int32, (C, C), 1)).astype(BF16)
    off = pl.multiple_of(n * C, C)
    col = lax.broadcasted_iota(jnp.int32, (PAIR, BAND), 1)
    valid = col >= BAND_PAD - n * C
    pairs = [slice(p * PAIR, (p + 1) * PAIR) for p in range(RWKV_HEADS // 2)]

    def rope(t):
        partner = jnp.where(first_half, pltpu.roll(t, PAIR - HEAD_DIM // 2, 1),
                            pltpu.roll(t, HEAD_DIM // 2, 1))
        return t * cos_t + partner * sin_t

    def prepare(i, sink):
        zc = zcols(i, 0, RWKV_COLS)
        prev = jnp.where(row == 0, carry_ref[i, 7:8, :], pltpu.roll(zc, 1, 0))
        carry_ref[i] = zc[C - 8:C, :]
        zs = zc + (prev - zc) * mu_ref[...]
        r = zs[:, 0:RWKV_W]
        k = zs[:, RWKV_W:2 * RWKV_W]
        v = zs[:, 2 * RWKV_W:3 * RWKV_W]
        wa_l = zs[:, 3 * RWKV_W:3 * RWKV_W + PAIR]
        g_l = zs[:, 3 * RWKV_W + PAIR:RWKV_COLS]

        wa_in = jnp.where(head0, jnp.tanh(wa_l), wa_l)
        lora = _dot_x3(wa_in, wwa_hi_ref[...], wwa_lo_ref[...])
        w_pre = w0 + lora[:, 0:RWKV_W]
        a = _sigmoid(a0 + lora[:, RWKV_W:2 * RWKV_W])
        g = jnp.dot(_sigmoid(g_l).astype(BF16), gup_ref[...], preferred_element_type=F32)
        log_decay = DECAY_SCALE * _sigmoid(w_pre)
        yield

        ld2 = jnp.concatenate(_split2(log_decay), axis=1)
        cs = jnp.dot(tri, ld2, preferred_element_type=F32)
        cum = cs[:, 0:RWKV_W] + cs[:, RWKV_W:2 * RWKV_W]
        e_incl = jnp.exp(cum)
        e_excl = jnp.exp(cum - log_decay)
        e_inv = jnp.exp(-cum)
        yield

        kk = k * k_k
        k2 = k * (1.0 + (a - 1.0) * k_a)
        rkr = r * k2 * r_k
        sums = head_sums([kk[:, sl] * kk[:, sl] for sl in pairs] + [rkr[:, sl] for sl in pairs])
        kk_ss, rkr_sum = sums[0:3], sums[3:6]
        sink["vals"] = dict(v=v, g=g, rkr_sum=rkr_sum)
        sink["chains"] = chains = []
        yield

        for p, sl in enumerate(pairs):
            kk_p = kk[:, sl] / jnp.maximum(jnp.sqrt(kk_ss[p]), 1e-12)
            a_hat = -kk_p * e_excl[:, sl]
            r_hat = r[:, sl] * e_incl[:, sl]
            b_til = blk(kk_p * a[:, sl] * e_inv[:, sl])
            k_til = blk(k2[:, sl] * e_inv[:, sl])
            chains.append(dict(
                i=i, p=p,
                ar=jnp.concatenate([a_hat, r_hat], axis=0).astype(BF16),
                bk=jnp.concatenate([b_til, k_til], axis=0).astype(BF16),
                v_b=blk(v[:, sl]).astype(BF16),
                p_end=e_incl[C - 1:C, sl]))
            yield

    y_ret = {}
    gates = {}
    y_att = {}

    def retention_tasks(i, p):
        st = {}
        sl = pairs[p]
        base = RET_OFF + p * PAIR

        def scores():
            q = rope(zcols(i, base, PAIR))
            kr = rope(zcols(i, base + RET_W, PAIR)) * (HEAD_DIM ** -0.5)
            gg = zcols(i, base + 3 * RET_W, PAIR)
            gates[i, p] = gg * _sigmoid(gg)
            st["q_dec"] = q * qdec_ref[:, sl]
            st["kd_b"] = blk(kr * kdec_ref[:, sl])
            st["v_b"] = blk(z_ref[i, :, base + 2 * RET_W:base + 2 * RET_W + PAIR].astype(BF16))
            st["scores"] = _dot_nt(q, blk(kr)) * dmat_ref[p]

        def output():
            r_old = r_ref[i, p]
            lhs = jnp.concatenate([st["scores"], st["q_dec"]], axis=1)
            rhs = jnp.concatenate([st["v_b"], r_old.astype(BF16)], axis=0)
            y_ret[i, p] = _dot(lhs, rhs)
            r_ref[i, p] = r_old * chunk_decay[:, sl] + _dot_tn(st["kd_b"], st["v_b"])

        return [scores, output]

    def attention_tasks(i, p):
        st = {}
        base = ATT_OFF + p * PAIR

        def scores():
            q = zcols(i, base, PAIR) * (HEAD_DIM ** -0.5)
            kbuf_ref[i, p, pl.ds(BAND_PAD + off, C), :] = (
                z_ref[i, :, base + ATT_W:base + ATT_W + PAIR].astype(BF16))
            vbuf_ref[i, p, pl.ds(BAND_PAD + off, C), :] = (
                z_ref[i, :, base + 2 * ATT_W:base + 2 * ATT_W + PAIR].astype(BF16))
            k_band = kbuf_ref[i, p, pl.ds(off, BAND), :]
            s = _dot_nt(blk(q), k_band) + bias_ref[p]
            s = jnp.where(valid, s, -1e30)
            m = jnp.max(s, axis=-1, keepdims=True)
            st["e"] = jnp.exp(s - m)
            st["l"] = jnp.sum(st["e"], axis=-1, keepdims=True)

        def output():
            v_band = vbuf_ref[i, p, pl.ds(off, BAND), :]
            o = _dot(st["e"], v_band) / st["l"]
            y_att[i, p] = jnp.where(head0, o[0:C], o[C:2 * C])

        return [scores, output]

    y_rwkv = {}

    def chain_stages(chains):
        for c in chains:
            c["s_old"] = s_ref[c["i"], c["p"]]
            big = _dot_nt(c["ar"], c["bk"])
            c["pw"] = jnp.where(cat_strict, big[0:C, 0:PAIR], 0.0).astype(BF16)
            c["a_ak"] = jnp.where(cat_strict, big[0:C, PAIR:2 * PAIR], 0.0)
            c["l_rbk"] = jnp.concatenate(
                [jnp.where(cat_incl, big[C:2 * C, 0:PAIR], 0.0),
                 jnp.where(cat_incl, big[C:2 * C, PAIR:2 * PAIR], 0.0)], axis=1).astype(BF16)
        yield
        for c in chains:
            c["ars"] = _dot_nt(c["ar"], c["s_old"])
        yield
        for c in chains:
            c["x"] = c["ars"][0:C] + _dot(c["a_ak"], c["v_b"])
        yield
        for _ in range(5):
            for c in chains:
                rhs = jnp.concatenate([blk(c["x"]).astype(BF16), blk(c["pw"])], axis=1)
                res = jnp.dot(c["pw"], rhs, preferred_element_type=F32)
                c["x"] = c["x"] + res[:, 0:PAIR]
                c["pw"] = res[:, PAIR:2 * PAIR].astype(BF16)
            yield
        for c in chains:
            c["x"] = c["x"] + jnp.dot(c["pw"], blk(c["x"]).astype(BF16), preferred_element_type=F32)
            c["uv"] = jnp.concatenate([blk(c["x"]).astype(BF16), c["v_b"]], axis=0)
        yield
        for c in chains:
            y_rwkv[c["i"], c["p"]] = c["ars"][C:2 * C] + jnp.dot(c["l_rbk"], c["uv"], preferred_element_type=F32)
        yield
        for c in chains:
            s_ref[c["i"], c["p"]] = (c["s_old"] + _dot_tn(c["uv"], c["bk"])) * c["p_end"]
        yield

    seqs = range(MIX_BATCH)
    ret_tasks = [retention_tasks(i, p) for i in seqs for p in range(RET_HEADS // 2)]
    att_tasks = [attention_tasks(i, p) for i in seqs for p in range(ATT_HEADS // 2)]

    def run(tasks, count):
        for _ in range(count):
            if tasks:
                tasks.pop(0)()

    ret_queue = [t[0] for t in ret_tasks] + [t[1] for t in ret_tasks]
    sinks = {i: {} for i in seqs}
    for i in seqs:
        for _ in prepare(i, sinks[i]):
            run(ret_queue, PREP_FILL)
    run(ret_queue, len(ret_queue))
    att_scores = [t[0] for t in att_tasks]
    att_outputs = [t[1] for t in att_tasks]
    for stage, _ in enumerate(chain_stages([c for i in seqs for c in sinks[i]["chains"]])):
        run(att_scores, 1)
        if stage >= ATT_LAG:
            run(att_outputs, 1)
    run(att_scores, len(att_scores))
    run(att_outputs, len(att_outputs))
    seq_vals = {i: sinks[i]["vals"] for i in seqs}

    normed_all = head_norms(
        [y for i in seqs for y in ([y_rwkv[i, p] for p in range(3)] + [y_ret[i, p] for p in range(3)])],
        ([ln_g[:, sl] for sl in pairs] + [gn_g[:, sl] for sl in pairs]) * MIX_BATCH,
        ([ln_b[:, sl] for sl in pairs] + [gn_b[:, sl] for sl in pairs]) * MIX_BATCH,
        ([RWKV_GN_EPS] * 3 + [RET_GN_EPS] * 3) * MIX_BATCH)
    for i in seqs:
        vals = seq_vals[i]
        normed = normed_all[6 * i:6 * i + 6]
        out = []
        for p, sl in enumerate(pairs):
            out.append((normed[p] + vals["rkr_sum"][p] * vals["v"][:, sl]) * vals["g"][:, sl])
        for p in range(RET_HEADS // 2):
            out.append(normed[3 + p] * gates[i, p])
        for p in range(ATT_HEADS // 2):
            out.append(y_att[i, p])
        y_ref[i] = jnp.concatenate(out, axis=1)


def _mixers(z, mix_params, consts, batch, seq):
    nc = seq // CHUNK
    nb = MIX_BATCH
    c2 = lambda b, n: (0, 0)
    c3 = lambda b, n: (0, 0, 0)
    (mu, rwv, wwa_hi, wwa_lo, gup, retv, bias) = mix_params
    (cos_t, sin_t, dmat, qdec, kdec) = consts
    in_specs = [
        pl.BlockSpec((nb, CHUNK, N_IN_COLS), lambda b, n: (b, n, 0)),
        pl.BlockSpec((1, RWKV_COLS), c2),
        pl.BlockSpec((8, RWKV_W), c2),
        pl.BlockSpec((PAIR, 2 * RWKV_W), c2),
        pl.BlockSpec((PAIR, 2 * RWKV_W), c2),
        pl.BlockSpec((GATE_LORA, RWKV_W), c2),
        pl.BlockSpec((8, RET_W), c2),
        pl.BlockSpec((CHUNK, PAIR), lambda b, n: (n, 0)),
        pl.BlockSpec((CHUNK, PAIR), lambda b, n: (n, 0)),
        pl.BlockSpec((RET_HEADS // 2, CHUNK, PAIR), c3),
        pl.BlockSpec((CHUNK, RET_W), c2),
        pl.BlockSpec((CHUNK, RET_W), c2),
        pl.BlockSpec((ATT_HEADS // 2, PAIR, BAND), c3),
    ]
    return pl.pallas_call(
        _mixer_kernel,
        grid=(batch // nb, nc),
        in_specs=in_specs,
        out_specs=pl.BlockSpec((nb, CHUNK, D_MODEL), lambda b, n: (b, n, 0)),
        out_shape=jax.ShapeDtypeStruct((batch, seq, D_MODEL), F32),
        scratch_shapes=[
            pltpu.VMEM((nb, RWKV_HEADS // 2, PAIR, PAIR), F32),
            pltpu.VMEM((nb, RET_HEADS // 2, PAIR, PAIR), F32),
            pltpu.VMEM((nb, 8, RWKV_COLS), F32),
            pltpu.VMEM((nb, ATT_HEADS // 2, BAND_PAD + seq, PAIR), BF16),
            pltpu.VMEM((nb, ATT_HEADS // 2, BAND_PAD + seq, PAIR), BF16),
        ],
        compiler_params=pltpu.CompilerParams(dimension_semantics=("parallel", "arbitrary"),
                                             vmem_limit_bytes=VMEM_LIMIT),
        name="mixers",
    )(z, mu, rwv, wwa_hi, wwa_lo, gup, retv, cos_t, sin_t, dmat, qdec, kdec, bias)


def _ffn_kernel(y_ref, x_ref, wout_ref, ln1g_ref, ln1b_ref, wup_ref, cw_ref, cb_ref, wdn_ref,
                ln2g_ref, ln2b_ref, o_ref, carry_ref, acc_ref, x1_ref, *slot_refs):
    tm = x_ref.shape[0]
    sub = min(FFN_SUB, tm)
    n_sub = tm // sub
    n_blocks = D_FF // FFN_COLS
    bw = 2 * FFN_COLS

    @pl.when(pl.program_id(1) == 0)
    def _():
        carry_ref[...] = jnp.zeros_like(carry_ref)

    u_refs = slot_refs[0:FFN_SLOTS]
    h_refs = slot_refs[FFN_SLOTS:2 * FFN_SLOTS]

    def rows_of(t):
        r = t // n_blocks
        return slice(r * sub, (r + 1) * sub)

    def head(r):
        rows = slice(r * sub, (r + 1) * sub)
        h = ALPHA * x_ref[rows, :] + jnp.dot(y_ref[rows, :].astype(BF16), wout_ref[...],
                                              preferred_element_type=F32)
        x1 = _layer_norm(h, ln1g_ref[...], ln1b_ref[...])
        x1_ref[rows, :] = x1.astype(BF16)
        acc_ref[rows, :] = ALPHA * x1

    def up(t):
        j = t % n_blocks
        u_ref = u_refs[t % FFN_SLOTS]
        cols = slice(j * bw, (j + 1) * bw)
        gcols = slice(j * FFN_COLS, (j + 1) * FFN_COLS)
        vcols = slice(D_FF + j * FFN_COLS, D_FF + (j + 1) * FFN_COLS)
        x1b = x1_ref[rows_of(t), :]
        u_ref[0:8, :] = carry_ref[:, cols]
        u_ref[8:8 + sub, 0:FFN_COLS] = jnp.dot(x1b, wup_ref[:, gcols], preferred_element_type=F32)
        u_ref[8:8 + sub, FFN_COLS:bw] = jnp.dot(x1b, wup_ref[:, vcols], preferred_element_type=F32)
        carry_ref[:, cols] = u_ref[sub:sub + 8, :]

    def gated(t):
        j = t % n_blocks
        u_ref = u_refs[t % FFN_SLOTS]
        halves = []
        for part in range(2):
            cols = slice(part * D_FF + j * FFN_COLS, part * D_FF + (j + 1) * FFN_COLS)
            ucols = slice(part * FFN_COLS, (part + 1) * FFN_COLS)
            halves.append(cb_ref[:, cols] + cw_ref[0:1, cols] * u_ref[6:6 + sub, ucols]
                          + cw_ref[1:2, cols] * u_ref[7:7 + sub, ucols]
                          + cw_ref[2:3, cols] * u_ref[8:8 + sub, ucols])
        gate, val = halves
        h_refs[t % FFN_SLOTS][...] = (gate * _sigmoid(gate) * val).astype(BF16)

    def down(t):
        j = t % n_blocks
        acc_ref[rows_of(t), :] += jnp.dot(h_refs[t % FFN_SLOTS][...], wdn_ref[j * FFN_COLS:(j + 1) * FFN_COLS, :],
                                          preferred_element_type=F32)

    def tail(r):
        rows = slice(r * sub, (r + 1) * sub)
        o_ref[rows, :] = _layer_norm(acc_ref[rows, :], ln2g_ref[...], ln2b_ref[...])

    total = n_sub * n_blocks
    head(0)
    for t in range(min(FFN_LOOKAHEAD, total)):
        up(t)
    for t in range(total + 1):
        nxt = t + FFN_LOOKAHEAD + 1
        if nxt < total and nxt % n_blocks == 0:
            head(nxt // n_blocks)
        if t + FFN_LOOKAHEAD < total:
            up(t + FFN_LOOKAHEAD)
        if t < total:
            gated(t)
        if t >= 1:
            down(t - 1)
            if t % n_blocks == 0:
                tail(t // n_blocks - 1)


def _out_ffn(y, x, wout, ln1g, ln1b, wup, cw, cb, wdn, ln2g, ln2b):
    batch, seq, _ = x.shape
    tm = min(ROW_TILE, seq)
    rows = lambda b, s: (b, s, 0)
    c2 = lambda b, s: (0, 0)
    one = pl.Buffered(1)
    return pl.pallas_call(
        _ffn_kernel,
        grid=(batch, seq // tm),
        in_specs=[
            pl.BlockSpec((None, tm, D_MODEL), rows),
            pl.BlockSpec((None, tm, D_MODEL), rows),
            pl.BlockSpec((D_MODEL, D_MODEL), c2, pipeline_mode=one),
            pl.BlockSpec((1, D_MODEL), c2),
            pl.BlockSpec((1, D_MODEL), c2),
            pl.BlockSpec((D_MODEL, 2 * D_FF), c2, pipeline_mode=one),
            pl.BlockSpec((CONV_W, 2 * D_FF), c2),
            pl.BlockSpec((1, 2 * D_FF), c2),
            pl.BlockSpec((D_FF, D_MODEL), c2, pipeline_mode=one),
            pl.BlockSpec((1, D_MODEL), c2),
            pl.BlockSpec((1, D_MODEL), c2),
        ],
        out_specs=pl.BlockSpec((None, tm, D_MODEL), rows),
        out_shape=jax.ShapeDtypeStruct((batch, seq, D_MODEL), F32),
        scratch_shapes=[pltpu.VMEM((8, 2 * D_FF), F32),
                        pltpu.VMEM((tm, D_MODEL), F32),
                        pltpu.VMEM((tm, D_MODEL), BF16),
                        *[pltpu.VMEM((min(FFN_SUB, tm) + 8, 2 * FFN_COLS), F32) for _ in range(FFN_SLOTS)],
                        *[pltpu.VMEM((min(FFN_SUB, tm), FFN_COLS), BF16) for _ in range(FFN_SLOTS)]],
        compiler_params=pltpu.CompilerParams(dimension_semantics=("parallel", "arbitrary"),
                                             vmem_limit_bytes=VMEM_LIMIT),
        name="out_ffn",
    )(y, x, wout, ln1g, ln1b, wup, cw, cb, wdn, ln2g, ln2b)


def _hi_lo(w):
    hi = w.astype(BF16)
    return hi, (w - hi.astype(F32)).astype(BF16)


def _mixer_constants(seq):
    half = HEAD_DIM // 2
    lane = jnp.arange(PAIR)
    inv = ROPE_BASE ** (-jnp.arange(half, dtype=F32) / half)
    ang = jnp.arange(seq, dtype=F32)[:, None] * inv[None, :]
    idx = lane % half
    cos_t = jnp.cos(ang)[:, idx]
    sign = jnp.where((lane % HEAD_DIM) < half, -1.0, 1.0).astype(F32)
    sin_t = jnp.sin(ang)[:, idx] * sign[None, :]

    log_g = jnp.log(1.0 - jnp.exp2(-5.0 - jnp.arange(RET_HEADS, dtype=F32)))
    t = jnp.arange(CHUNK, dtype=F32)
    lane_s = (lane % CHUNK).astype(F32)
    lane_h = lane // CHUNK
    pairs = jnp.arange(RET_HEADS // 2)
    lg_cat = log_g[2 * pairs[:, None] + lane_h[None, :]]
    dist = jnp.abs(t[:, None] - lane_s[None, :])
    dmat = jnp.exp(lg_cat[:, None, :] * dist[None])
    lg_lane = jnp.repeat(log_g, HEAD_DIM)
    qdec = jnp.exp(lg_lane[None, :] * (t + 1.0)[:, None])
    kdec = jnp.exp(lg_lane[None, :] * (CHUNK - 1.0 - t)[:, None])
    chunk_decay = jnp.exp(lg_lane * CHUNK)
    return (cos_t, sin_t, dmat.astype(F32), qdec.astype(F32), kdec.astype(F32)), chunk_decay


def _attention_bias(rel_bias):
    t = jnp.arange(BAND + CHUNK - 1)
    ext_idx = jnp.clip(BAND_PAD + (CHUNK - 1) - t, -(CHUNK - 1), REL_CLIP) + (CHUNK - 1)
    ext = rel_bias.astype(F32)[:, ext_idx]
    bias = jnp.stack([ext[:, CHUNK - 1 - i:CHUNK - 1 - i + BAND] for i in range(CHUNK)], axis=1)
    return bias.reshape(ATT_HEADS // 2, PAIR, BAND)


def kernel(x, ln_in_g, ln_in_b, w_in, rw_mu, rw_w0, rw_w_up, rw_a0, rw_a_up, rw_g_up, rw_k_k, rw_k_a, rw_r_k, rw_ln_g, rw_ln_b, ret_gn_g, ret_gn_b, attn_rel_bias, w_out, ln1_g, ln1_b, ffn_w_up, ffn_conv_w, ffn_conv_b, ffn_w_down, ln2_g, ln2_b):
    batch, seq, _ = x.shape
    consts, chunk_decay = _mixer_constants(seq)
    row2 = lambda t: t.reshape(1, -1).astype(F32)
    xcur = x.reshape(batch * seq, D_MODEL)
    for l in range(DEPTH):
        if l == 0:
            z, xcur = _project(xcur, row2(ln_in_g), row2(ln_in_b), w_in[l].astype(BF16), True)
        else:
            (z,) = _project(xcur, row2(ln_in_g), row2(ln_in_b), w_in[l].astype(BF16), False)

        zero_row = jnp.zeros((RWKV_W,), F32)
        rwv = jnp.stack([rw_w0[l], rw_a0[l], rw_k_k[l], rw_k_a[l], rw_r_k[l].reshape(-1),
                         rw_ln_g[l], rw_ln_b[l], zero_row]).astype(F32)
        wwa = jnp.zeros((PAIR, 2 * RWKV_W), F32)
        wwa = wwa.at[0:DECAY_LORA, 0:RWKV_W].set(rw_w_up[l].astype(F32))
        wwa = wwa.at[DECAY_LORA:PAIR, RWKV_W:2 * RWKV_W].set(rw_a_up[l].astype(F32))
        wwa_hi, wwa_lo = _hi_lo(wwa)
        retv = jnp.concatenate([jnp.stack([ret_gn_g[l].astype(F32), ret_gn_b[l].astype(F32), chunk_decay]),
                                jnp.zeros((5, RET_W), F32)])
        mix_params = (row2(rw_mu[l]), rwv, wwa_hi, wwa_lo, rw_g_up[l].astype(BF16), retv,
                      _attention_bias(attn_rel_bias[l]))
        y = _mixers(z.reshape(batch, seq, N_IN_COLS), mix_params, consts, batch, seq)

        xcur = _out_ffn(y, xcur.reshape(batch, seq, D_MODEL), w_out[l].astype(BF16),
                        row2(ln1_g[l]), row2(ln1_b[l]), ffn_w_up[l].astype(BF16),
                        ffn_conv_w[l].astype(F32), row2(ffn_conv_b[l]), ffn_w_down[l].astype(BF16),
                        row2(ln2_g[l]), row2(ln2_b[l])).reshape(batch * seq, D_MODEL)
    return xcur.reshape(batch, seq, D_MODEL)
```

```python
import functools

import jax
import jax.numpy as jnp
from jax import lax
from jax.experimental import pallas as pl
from jax.experimental.pallas import tpu as pltpu

F32 = jnp.float32
BF16 = jnp.bfloat16

D_MODEL = 1024
DEPTH = 2
CHUNK = 64
HEAD_DIM = 64
PAIR = 2 * HEAD_DIM
RWKV_HEADS, RET_HEADS, ATT_HEADS = 6, 6, 4
RWKV_W, RET_W, ATT_W = 384, 384, 256
DECAY_LORA, AAA_LORA, GATE_LORA = 64, 64, 128
RWKV_COLS = 3 * RWKV_W + DECAY_LORA + AAA_LORA + GATE_LORA
RET_COLS = 4 * RET_W
ATT_COLS = 3 * ATT_W
N_IN_COLS = RWKV_COLS + RET_COLS + ATT_COLS
RET_OFF = RWKV_COLS
ATT_OFF = RWKV_COLS + RET_COLS
BAND_PREV_CHUNKS = 8
BAND_PAD = BAND_PREV_CHUNKS * CHUNK
BAND = BAND_PAD + CHUNK
REL_CLIP = 128
D_FF = 2816
CONV_W = 3
ALPHA = (2 * DEPTH) ** 0.25
ROPE_BASE = 10000.0
LN_EPS = 1e-5
RWKV_GN_EPS = 64e-5
RET_GN_EPS = 1e-5
DECAY_SCALE = -0.6065306597126334

ROW_TILE = 512
FFN_COLS = 256
FFN_SUB = 512
FFN_LOOKAHEAD = 2
FFN_SLOTS = FFN_LOOKAHEAD + 1
MIX_BATCH = 8
PREP_FILL = 1
ATT_LAG = 2
VMEM_LIMIT = 56 * 1024 * 1024

assert CHUNK == HEAD_DIM


def _dot(a, b):
    return jnp.dot(a.astype(BF16), b.astype(BF16), preferred_element_type=F32)


def _dot_nt(a, b):
    return lax.dot_general(a.astype(BF16), b.astype(BF16), (((1,), (1,)), ((), ())),
                           preferred_element_type=F32)


def _dot_tn(a, b):
    return lax.dot_general(a.astype(BF16), b.astype(BF16), (((0,), (0,)), ((), ())),
                           preferred_element_type=F32)


def _split2(x):
    hi = x.astype(BF16)
    lo = (x - hi.astype(F32)).astype(BF16)
    return hi, lo


def _split3(x):
    hi = x.astype(BF16)
    r1 = x - hi.astype(F32)
    mid = r1.astype(BF16)
    lo = (r1 - mid.astype(F32)).astype(BF16)
    return hi, mid, lo


def _dot_x3(x, w_hi, w_lo):
    x_hi, x_lo = _split2(x)
    return (jnp.dot(x_hi, w_hi, preferred_element_type=F32)
            + jnp.dot(x_hi, w_lo, preferred_element_type=F32)
            + jnp.dot(x_lo, w_hi, preferred_element_type=F32))


def _layer_norm(x, g, b, eps=LN_EPS):
    mu = jnp.mean(x, axis=-1, keepdims=True)
    d = x - mu
    var = jnp.mean(d * d, axis=-1, keepdims=True)
    return d * lax.rsqrt(var + eps) * g + b


def _sigmoid(x):
    return 1.0 / (1.0 + jnp.exp(-x))


def _proj_kernel(apply_ln, x_ref, g_ref, b_ref, w_ref, z_ref, *xn_ref):
    x = x_ref[...]
    if apply_ln:
        x = _layer_norm(x, g_ref[...], b_ref[...])
        xn_ref[0][...] = x
    z_ref[...] = jnp.dot(x.astype(BF16), w_ref[...], preferred_element_type=F32)


def _project(x2d, ln_g, ln_b, w_bf16, apply_ln):
    m = x2d.shape[0]
    tm = ROW_TILE
    row = lambda i: (i, 0)
    const = lambda i: (0, 0)
    out_shape = [jax.ShapeDtypeStruct((m, N_IN_COLS), F32)]
    out_specs = [pl.BlockSpec((tm, N_IN_COLS), row)]
    if apply_ln:
        out_shape.append(jax.ShapeDtypeStruct((m, D_MODEL), F32))
        out_specs.append(pl.BlockSpec((tm, D_MODEL), row))
    outs = pl.pallas_call(
        functools.partial(_proj_kernel, apply_ln),
        grid=(m // tm,),
        in_specs=[pl.BlockSpec((tm, D_MODEL), row),
                  pl.BlockSpec((1, D_MODEL), const),
                  pl.BlockSpec((1, D_MODEL), const),
                  pl.BlockSpec((D_MODEL, N_IN_COLS), const, pipeline_mode=pl.Buffered(1))],
        out_specs=out_specs,
        out_shape=out_shape,
        compiler_params=pltpu.CompilerParams(dimension_semantics=("parallel",),
                                             vmem_limit_bytes=VMEM_LIMIT),
        name="in_proj",
    )(x2d, ln_g, ln_b, w_bf16)
    return outs


def _mixer_kernel(z_ref, mu_ref, rwv_ref, wwa_hi_ref, wwa_lo_ref, gup_ref,
                  retv_ref, cos_ref, sin_ref, dmat_ref, qdec_ref, kdec_ref, bias_ref,
                  y_ref, s_ref, r_ref, carry_ref, kbuf_ref, vbuf_ref):
    n = pl.program_id(1)
    C = CHUNK

    @pl.when(n == 0)
    def _():
        s_ref[...] = jnp.zeros_like(s_ref)
        r_ref[...] = jnp.zeros_like(r_ref)
        carry_ref[...] = jnp.zeros_like(carry_ref)
        kbuf_ref[:, :, 0:BAND_PAD, :] = jnp.zeros((MIX_BATCH, ATT_HEADS // 2, BAND_PAD, PAIR), BF16)
        vbuf_ref[:, :, 0:BAND_PAD, :] = jnp.zeros((MIX_BATCH, ATT_HEADS // 2, BAND_PAD, PAIR), BF16)

    lane = lax.broadcasted_iota(jnp.int32, (C, PAIR), 1)
    time = lax.broadcasted_iota(jnp.int32, (C, PAIR), 0)
    head0 = lane < HEAD_DIM
    cat_incl = time >= (lane % C)
    cat_strict = time > (lane % C)

    def blk(x):
        zero = jnp.zeros_like(x)
        return jnp.concatenate([jnp.where(head0, x, zero), jnp.where(head0, zero, x)], axis=0)

    def zcols(i, start, width):
        return z_ref[i, :, start:start + width]

    r2 = lax.broadcasted_iota(jnp.int32, (2 * PAIR, 2 * PAIR), 0)
    c2 = lax.broadcasted_iota(jnp.int32, (2 * PAIR, 2 * PAIR), 1)
    head_ones = ((r2 // HEAD_DIM) == (c2 // HEAD_DIM)).astype(BF16)

    def head_sums(slabs):
        x = jnp.concatenate([jnp.concatenate(slabs[i:i + 2], axis=1) for i in range(0, len(slabs), 2)], axis=0)
        s = jnp.dot(x.astype(BF16), head_ones, preferred_element_type=F32)
        return [s[(i // 2) * C:(i // 2 + 1) * C, (i % 2) * PAIR:(i % 2 + 1) * PAIR] for i in range(len(slabs))]

    def head_norms(ys, gs, bs, epss):
        mus = head_sums(ys)
        ds = [y - mu * (1.0 / HEAD_DIM) for y, mu in zip(ys, mus)]
        vs = head_sums([d * d for d in ds])
        return [d * lax.rsqrt(var * (1.0 / HEAD_DIM) + eps) * g + b
                for d, var, g, b, eps in zip(ds, vs, gs, bs, epss)]

    w0 = rwv_ref[0:1, :]
    a0 = rwv_ref[1:2, :]
    k_k = rwv_ref[2:3, :]
    k_a = rwv_ref[3:4, :]
    r_k = rwv_ref[4:5, :]
    ln_g = rwv_ref[5:6, :]
    ln_b = rwv_ref[6:7, :]
    gn_g = retv_ref[0:1, :]
    gn_b = retv_ref[1:2, :]
    chunk_decay = retv_ref[2:3, :]
    cos_t = cos_ref[...]
    sin_t = sin_ref[...]
    first_half = (lane % HEAD_DIM) < (HEAD_DIM // 2)
    row = lax.broadcasted_iota(jnp.int32, (C, RWKV_COLS), 0)
    tri = (lax.broadcasted_iota(jnp.int32, (C, C), 0)
           >= lax.broadcasted_iota(jnp.int32, (C, C), 1)).astype(BF16)
    off = pl.multiple_of(n * C, C)
    col = lax.broadcasted_iota(jnp.int32, (PAIR, BAND), 1)
    valid = col >= BAND_PAD - n * C
    pairs = [slice(p * PAIR, (p + 1) * PAIR) for p in range(RWKV_HEADS // 2)]

    def rope(t):
        partner = jnp.where(first_half, pltpu.roll(t, PAIR - HEAD_DIM // 2, 1),
                            pltpu.roll(t, HEAD_DIM // 2, 1))
        return t * cos_t + partner * sin_t

    def prepare(i, sink):
        zc = zcols(i, 0, RWKV_COLS)
        prev = jnp.where(row == 0, carry_ref[i, 7:8, :], pltpu.roll(zc, 1, 0))
        carry_ref[i] = zc[C - 8:C, :]
        zs = zc + (prev - zc) * mu_ref[...]
        r = zs[:, 0:RWKV_W]
        k = zs[:, RWKV_W:2 * RWKV_W]
        v = zs[:, 2 * RWKV_W:3 * RWKV_W]
        wa_l = zs[:, 3 * RWKV_W:3 * RWKV_W + PAIR]
        g_l = zs[:, 3 * RWKV_W + PAIR:RWKV_COLS]

        wa_in = jnp.where(head0, jnp.tanh(wa_l), wa_l)
        lora = _dot_x3(wa_in, wwa_hi_ref[...], wwa_lo_ref[...])
        w_pre = w0 + lora[:, 0:RWKV_W]
        a = _sigmoid(a0 + lora[:, RWKV_W:2 * RWKV_W])
        g = jnp.dot(_sigmoid(g_l).astype(BF16), gup_ref[...], preferred_element_type=F32)
        log_decay = DECAY_SCALE * _sigmoid(w_pre)
        yield

        ld2 = jnp.concatenate(_split2(log_decay), axis=1)
        cs = jnp.dot(tri, ld2, preferred_element_type=F32)
        cum = cs[:, 0:RWKV_W] + cs[:, RWKV_W:2 * RWKV_W]
        e_incl = jnp.exp(cum)
        e_excl = jnp.exp(cum - log_decay)
        e_inv = jnp.exp(-cum)
        yield

        kk = k * k_k
        k2 = k * (1.0 + (a - 1.0) * k_a)
        rkr = r * k2 * r_k
        sums = head_sums([kk[:, sl] * kk[:, sl] for sl in pairs] + [rkr[:, sl] for sl in pairs])
        kk_ss, rkr_sum = sums[0:3], sums[3:6]
        sink["vals"] = dict(v=v, g=g, rkr_sum=rkr_sum)
        sink["chains"] = chains = []
        yield

        for p, sl in enumerate(pairs):
            kk_p = kk[:, sl] / jnp.maximum(jnp.sqrt(kk_ss[p]), 1e-12)
            a_hat = -kk_p * e_excl[:, sl]
            r_hat = r[:, sl] * e_incl[:, sl]
            b_til = blk(kk_p * a[:, sl] * e_inv[:, sl])
            k_til = blk(k2[:, sl] * e_inv[:, sl])
            chains.append(dict(
                i=i, p=p,
                ar=jnp.concatenate([a_hat, r_hat], axis=0).astype(BF16),
                bk=jnp.concatenate([b_til, k_til], axis=0).astype(BF16),
                v_b=blk(v[:, sl]).astype(BF16),
                p_end=e_incl[C - 1:C, sl]))
            yield

    y_ret = {}
    gates = {}
    y_att = {}

    def retention_tasks(i, p):
        st = {}
        sl = pairs[p]
        base = RET_OFF + p * PAIR

        def scores():
            q = rope(zcols(i, base, PAIR))
            kr = rope(zcols(i, base + RET_W, PAIR)) * (HEAD_DIM ** -0.5)
            gg = zcols(i, base + 3 * RET_W, PAIR)
            gates[i, p] = gg * _sigmoid(gg)
            st["q_dec"] = q * qdec_ref[:, sl]
            st["kd_b"] = blk(kr * kdec_ref[:, sl])
            st["v_b"] = blk(z_ref[i, :, base + 2 * RET_W:base + 2 * RET_W + PAIR].astype(BF16))
            st["scores"] = _dot_nt(q, blk(kr)) * dmat_ref[p]

        def output():
            r_old = r_ref[i, p]
            lhs = jnp.concatenate([st["scores"], st["q_dec"]], axis=1)
            rhs = jnp.concatenate([st["v_b"], r_old.astype(BF16)], axis=0)
            y_ret[i, p] = _dot(lhs, rhs)
            r_ref[i, p] = r_old * chunk_decay[:, sl] + _dot_tn(st["kd_b"], st["v_b"])

        return [scores, output]

    def attention_tasks(i, p):
        st = {}
        base = ATT_OFF + p * PAIR

        def scores():
            q = zcols(i, base, PAIR) * (HEAD_DIM ** -0.5)
            kbuf_ref[i, p, pl.ds(BAND_PAD + off, C), :] = (
                z_ref[i, :, base + ATT_W:base + ATT_W + PAIR].astype(BF16))
            vbuf_ref[i, p, pl.ds(BAND_PAD + off, C), :] = (
                z_ref[i, :, base + 2 * ATT_W:base + 2 * ATT_W + PAIR].astype(BF16))
            k_band = kbuf_ref[i, p, pl.ds(off, BAND), :]
            s = _dot_nt(blk(q), k_band) + bias_ref[p]
            s = jnp.where(valid, s, -1e30)
            m = jnp.max(s, axis=-1, keepdims=True)
            st["e"] = jnp.exp(s - m)
            st["l"] = jnp.sum(st["e"], axis=-1, keepdims=True)

        def output():
            v_band = vbuf_ref[i, p, pl.ds(off, BAND), :]
            o = _dot(st["e"], v_band) / st["l"]
            y_att[i, p] = jnp.where(head0, o[0:C], o[C:2 * C])

        return [scores, output]

    y_rwkv = {}

    def chain_stages(chains):
        for c in chains:
            c["s_old"] = s_ref[c["i"], c["p"]]
            big = _dot_nt(c["ar"], c["bk"])
            c["pw"] = jnp.where(cat_strict, big[0:C, 0:PAIR], 0.0).astype(BF16)
            c["a_ak"] = jnp.where(cat_strict, big[0:C, PAIR:2 * PAIR], 0.0)
            c["l_rbk"] = jnp.concatenate(
                [jnp.where(cat_incl, big[C:2 * C, 0:PAIR], 0.0),
                 jnp.where(cat_incl, big[C:2 * C, PAIR:2 * PAIR], 0.0)], axis=1).astype(BF16)
        yield
        for c in chains:
            c["ars"] = _dot_nt(c["ar"], c["s_old"])
        yield
        for c in chains:
            c["x"] = c["ars"][0:C] + _dot(c["a_ak"], c["v_b"])
        yield
        for _ in range(5):
            for c in chains:
                rhs = jnp.concatenate([blk(c["x"]).astype(BF16), blk(c["pw"])], axis=1)
                res = jnp.dot(c["pw"], rhs, preferred_element_type=F32)
                c["x"] = c["x"] + res[:, 0:PAIR]
                c["pw"] = res[:, PAIR:2 * PAIR].astype(BF16)
            yield
        for c in chains:
            c["x"] = c["x"] + jnp.dot(c["pw"], blk(c["x"]).astype(BF16), preferred_element_type=F32)
            c["uv"] = jnp.concatenate([blk(c["x"]).astype(BF16), c["v_b"]], axis=0)
        yield
        for c in chains:
            y_rwkv[c["i"], c["p"]] = c["ars"][C:2 * C] + jnp.dot(c["l_rbk"], c["uv"], preferred_element_type=F32)
        yield
        for c in chains:
            s_ref[c["i"], c["p"]] = (c["s_old"] + _dot_tn(c["uv"], c["bk"])) * c["p_end"]
        yield

    seqs = range(MIX_BATCH)
    ret_tasks = [retention_tasks(i, p) for i in seqs for p in range(RET_HEADS // 2)]
    att_tasks = [attention_tasks(i, p) for i in seqs for p in range(ATT_HEADS // 2)]

    def run(tasks, count):
        for _ in range(count):
            if tasks:
                tasks.pop(0)()

    ret_queue = [t[0] for t in ret_tasks] + [t[1] for t in ret_tasks]
    sinks = {i: {} for i in seqs}
    for i in seqs:
        for _ in prepare(i, sinks[i]):
            run(ret_queue, PREP_FILL)
    run(ret_queue, len(ret_queue))
    att_scores = [t[0] for t in att_tasks]
    att_outputs = [t[1] for t in att_tasks]
    for stage, _ in enumerate(chain_stages([c for i in seqs for c in sinks[i]["chains"]])):
        run(att_scores, 1)
        if stage >= ATT_LAG:
            run(att_outputs, 1)
    run(att_scores, len(att_scores))
    run(att_outputs, len(att_outputs))
    seq_vals = {i: sinks[i]["vals"] for i in seqs}

    normed_all = head_norms(
        [y for i in seqs for y in ([y_rwkv[i, p] for p in range(3)] + [y_ret[i, p] for p in range(3)])],
        ([ln_g[:, sl] for sl in pairs] + [gn_g[:, sl] for sl in pairs]) * MIX_BATCH,
        ([ln_b[:, sl] for sl in pairs] + [gn_b[:, sl] for sl in pairs]) * MIX_BATCH,
        ([RWKV_GN_EPS] * 3 + [RET_GN_EPS] * 3) * MIX_BATCH)
    for i in seqs:
        vals = seq_vals[i]
        normed = normed_all[6 * i:6 * i + 6]
        out = []
        for p, sl in enumerate(pairs):
            out.append((normed[p] + vals["rkr_sum"][p] * vals["v"][:, sl]) * vals["g"][:, sl])
        for p in range(RET_HEADS // 2):
            out.append(normed[3 + p] * gates[i, p])
        for p in range(ATT_HEADS // 2):
            out.append(y_att[i, p])
        y_ref[i] = jnp.concatenate(out, axis=1)


def _mixers(z, mix_params, consts, batch, seq):
    nc = seq // CHUNK
    nb = MIX_BATCH
    c2 = lambda b, n: (0, 0)
    c3 = lambda b, n: (0, 0, 0)
    (mu, rwv, wwa_hi, wwa_lo, gup, retv, bias) = mix_params
    (cos_t, sin_t, dmat, qdec, kdec) = consts
    in_specs = [
        pl.BlockSpec((nb, CHUNK, N_IN_COLS), lambda b, n: (b, n, 0)),
        pl.BlockSpec((1, RWKV_COLS), c2),
        pl.BlockSpec((8, RWKV_W), c2),
        pl.BlockSpec((PAIR, 2 * RWKV_W), c2),
        pl.BlockSpec((PAIR, 2 * RWKV_W), c2),
        pl.BlockSpec((GATE_LORA, RWKV_W), c2),
        pl.BlockSpec((8, RET_W), c2),
        pl.BlockSpec((CHUNK, PAIR), lambda b, n: (n, 0)),
        pl.BlockSpec((CHUNK, PAIR), lambda b, n: (n, 0)),
        pl.BlockSpec((RET_HEADS // 2, CHUNK, PAIR), c3),
        pl.BlockSpec((CHUNK, RET_W), c2),
        pl.BlockSpec((CHUNK, RET_W), c2),
        pl.BlockSpec((ATT_HEADS // 2, PAIR, BAND), c3),
    ]
    return pl.pallas_call(
        _mixer_kernel,
        grid=(batch // nb, nc),
        in_specs=in_specs,
        out_specs=pl.BlockSpec((nb, CHUNK, D_MODEL), lambda b, n: (b, n, 0)),
        out_shape=jax.ShapeDtypeStruct((batch, seq, D_MODEL), F32),
        scratch_shapes=[
            pltpu.VMEM((nb, RWKV_HEADS // 2, PAIR, PAIR), F32),
            pltpu.VMEM((nb, RET_HEADS // 2, PAIR, PAIR), F32),
            pltpu.VMEM((nb, 8, RWKV_COLS), F32),
            pltpu.VMEM((nb, ATT_HEADS // 2, BAND_PAD + seq, PAIR), BF16),
            pltpu.VMEM((nb, ATT_HEADS // 2, BAND_PAD + seq, PAIR), BF16),
        ],
        compiler_params=pltpu.CompilerParams(dimension_semantics=("parallel", "arbitrary"),
                                             vmem_limit_bytes=VMEM_LIMIT),
        name="mixers",
    )(z, mu, rwv, wwa_hi, wwa_lo, gup, retv, cos_t, sin_t, dmat, qdec, kdec, bias)


def _ffn_kernel(y_ref, x_ref, wout_ref, ln1g_ref, ln1b_ref, wup_ref, cw_ref, cb_ref, wdn_ref,
                ln2g_ref, ln2b_ref, o_ref, carry_ref, acc_ref, x1_ref, *slot_refs):
    tm = x_ref.shape[0]
    sub = min(FFN_SUB, tm)
    n_sub = tm // sub
    n_blocks = D_FF // FFN_COLS
    bw = 2 * FFN_COLS

    @pl.when(pl.program_id(1) == 0)
    def _():
        carry_ref[...] = jnp.zeros_like(carry_ref)

    u_refs = slot_refs[0:FFN_SLOTS]
    h_refs = slot_refs[FFN_SLOTS:2 * FFN_SLOTS]

    def rows_of(t):
        r = t // n_blocks
        return slice(r * sub, (r + 1) * sub)

    def head(r):
        rows = slice(r * sub, (r + 1) * sub)
        h = ALPHA * x_ref[rows, :] + jnp.dot(y_ref[rows, :].astype(BF16), wout_ref[...],
                                              preferred_element_type=F32)
        x1 = _layer_norm(h, ln1g_ref[...], ln1b_ref[...])
        x1_ref[rows, :] = x1.astype(BF16)
        acc_ref[rows, :] = ALPHA * x1

    def up(t):
        j = t % n_blocks
        u_ref = u_refs[t % FFN_SLOTS]
        cols = slice(j * bw, (j + 1) * bw)
        gcols = slice(j * FFN_COLS, (j + 1) * FFN_COLS)
        vcols = slice(D_FF + j * FFN_COLS, D_FF + (j + 1) * FFN_COLS)
        x1b = x1_ref[rows_of(t), :]
        u_ref[0:8, :] = carry_ref[:, cols]
        u_ref[8:8 + sub, 0:FFN_COLS] = jnp.dot(x1b, wup_ref[:, gcols], preferred_element_type=F32)
        u_ref[8:8 + sub, FFN_COLS:bw] = jnp.dot(x1b, wup_ref[:, vcols], preferred_element_type=F32)
        carry_ref[:, cols] = u_ref[sub:sub + 8, :]

    def gated(t):
        j = t % n_blocks
        u_ref = u_refs[t % FFN_SLOTS]
        halves = []
        for part in range(2):
            cols = slice(part * D_FF + j * FFN_COLS, part * D_FF + (j + 1) * FFN_COLS)
            ucols = slice(part * FFN_COLS, (part + 1) * FFN_COLS)
            halves.append(cb_ref[:, cols] + cw_ref[0:1, cols] * u_ref[6:6 + sub, ucols]
                          + cw_ref[1:2, cols] * u_ref[7:7 + sub, ucols]
                          + cw_ref[2:3, cols] * u_ref[8:8 + sub, ucols])
        gate, val = halves
        h_refs[t % FFN_SLOTS][...] = (gate * _sigmoid(gate) * val).astype(BF16)

    def down(t):
        j = t % n_blocks
        acc_ref[rows_of(t), :] += jnp.dot(h_refs[t % FFN_SLOTS][...], wdn_ref[j * FFN_COLS:(j + 1) * FFN_COLS, :],
                                          preferred_element_type=F32)

    def tail(r):
        rows = slice(r * sub, (r + 1) * sub)
        o_ref[rows, :] = _layer_norm(acc_ref[rows, :], ln2g_ref[...], ln2b_ref[...])

    total = n_sub * n_blocks
    head(0)
    for t in range(min(FFN_LOOKAHEAD, total)):
        up(t)
    for t in range(total + 1):
        nxt = t + FFN_LOOKAHEAD + 1
        if nxt < total and nxt % n_blocks == 0:
            head(nxt // n_blocks)
        if t + FFN_LOOKAHEAD < total:
            up(t + FFN_LOOKAHEAD)
        if t < total:
            gated(t)
        if t >= 1:
            down(t - 1)
            if t % n_blocks == 0:
                tail(t // n_blocks - 1)


def _out_ffn(y, x, wout, ln1g, ln1b, wup, cw, cb, wdn, ln2g, ln2b):
    batch, seq, _ = x.shape
    tm = min(ROW_TILE, seq)
    rows = lambda b, s: (b, s, 0)
    c2 = lambda b, s: (0, 0)
    one = pl.Buffered(1)
    return pl.pallas_call(
        _ffn_kernel,
        grid=(batch, seq // tm),
        in_specs=[
            pl.BlockSpec((None, tm, D_MODEL), rows),
            pl.BlockSpec((None, tm, D_MODEL), rows),
            pl.BlockSpec((D_MODEL, D_MODEL), c2, pipeline_mode=one),
            pl.BlockSpec((1, D_MODEL), c2),
            pl.BlockSpec((1, D_MODEL), c2),
            pl.BlockSpec((D_MODEL, 2 * D_FF), c2, pipeline_mode=one),
            pl.BlockSpec((CONV_W, 2 * D_FF), c2),
            pl.BlockSpec((1, 2 * D_FF), c2),
            pl.BlockSpec((D_FF, D_MODEL), c2, pipeline_mode=one),
            pl.BlockSpec((1, D_MODEL), c2),
            pl.BlockSpec((1, D_MODEL), c2),
        ],
        out_specs=pl.BlockSpec((None, tm, D_MODEL), rows),
        out_shape=jax.ShapeDtypeStruct((batch, seq, D_MODEL), F32),
        scratch_shapes=[pltpu.VMEM((8, 2 * D_FF), F32),
                        pltpu.VMEM((tm, D_MODEL), F32),
                        pltpu.VMEM((tm, D_MODEL), BF16),
                        *[pltpu.VMEM((min(FFN_SUB, tm) + 8, 2 * FFN_COLS), F32) for _ in range(FFN_SLOTS)],
                        *[pltpu.VMEM((min(FFN_SUB, tm), FFN_COLS), BF16) for _ in range(FFN_SLOTS)]],
        compiler_params=pltpu.CompilerParams(dimension_semantics=("parallel", "arbitrary"),
                                             vmem_limit_bytes=VMEM_LIMIT),
        name="out_ffn",
    )(y, x, wout, ln1g, ln1b, wup, cw, cb, wdn, ln2g, ln2b)


def _hi_lo(w):
    hi = w.astype(BF16)
    return hi, (w - hi.astype(F32)).astype(BF16)


def _mixer_constants(seq):
    half = HEAD_DIM // 2
    lane = jnp.arange(PAIR)
    inv = ROPE_BASE ** (-jnp.arange(half, dtype=F32) / half)
    ang = jnp.arange(seq, dtype=F32)[:, None] * inv[None, :]
    idx = lane % half
    cos_t = jnp.cos(ang)[:, idx]
    sign = jnp.where((lane % HEAD_DIM) < half, -1.0, 1.0).astype(F32)
    sin_t = jnp.sin(ang)[:, idx] * sign[None, :]

    log_g = jnp.log(1.0 - jnp.exp2(-5.0 - jnp.arange(RET_HEADS, dtype=F32)))
    t = jnp.arange(CHUNK, dtype=F32)
    lane_s = (lane % CHUNK).astype(F32)
    lane_h = lane // CHUNK
    pairs = jnp.arange(RET_HEADS // 2)
    lg_cat = log_g[2 * pairs[:, None] + lane_h[None, :]]
    dist = jnp.abs(t[:, None] - lane_s[None, :])
    dmat = jnp.exp(lg_cat[:, None, :] * dist[None])
    lg_lane = jnp.repeat(log_g, HEAD_DIM)
    qdec = jnp.exp(lg_lane[None, :] * (t + 1.0)[:, None])
    kdec = jnp.exp(lg_lane[None, :] * (CHUNK - 1.0 - t)[:, None])
    chunk_decay = jnp.exp(lg_lane * CHUNK)
    return (cos_t, sin_t, dmat.astype(F32), qdec.astype(F32), kdec.astype(F32)), chunk_decay


def _attention_bias(rel_bias):
    t = jnp.arange(BAND + CHUNK - 1)
    ext_idx = jnp.clip(BAND_PAD + (CHUNK - 1) - t, -(CHUNK - 1), REL_CLIP) + (CHUNK - 1)
    ext = rel_bias.astype(F32)[:, ext_idx]
    bias = jnp.stack([ext[:, CHUNK - 1 - i:CHUNK - 1 - i + BAND] for i in range(CHUNK)], axis=1)
    return bias.reshape(ATT_HEADS // 2, PAIR, BAND)


def kernel(x, ln_in_g, ln_in_b, w_in, rw_mu, rw_w0, rw_w_up, rw_a0, rw_a_up, rw_g_up, rw_k_k, rw_k_a, rw_r_k, rw_ln_g, rw_ln_b, ret_gn_g, ret_gn_b, attn_rel_bias, w_out, ln1_g, ln1_b, ffn_w_up, ffn_conv_w, ffn_conv_b, ffn_w_down, ln2_g, ln2_b):
    batch, seq, _ = x.shape
    consts, chunk_decay = _mixer_constants(seq)
    row2 = lambda t: t.reshape(1, -1).astype(F32)
    xcur = x.reshape(batch * seq, D_MODEL)
    for l in range(DEPTH):
        if l == 0:
            z, xcur = _project(xcur, row2(ln_in_g), row2(ln_in_b), w_in[l].astype(BF16), True)
        else:
            (z,) = _project(xcur, row2(ln_in_g), row2(ln_in_b), w_in[l].astype(BF16), False)

        zero_row = jnp.zeros((RWKV_W,), F32)
        rwv = jnp.stack([rw_w0[l], rw_a0[l], rw_k_k[l], rw_k_a[l], rw_r_k[l].reshape(-1),
                         rw_ln_g[l], rw_ln_b[l], zero_row]).astype(F32)
        wwa = jnp.zeros((PAIR, 2 * RWKV_W), F32)
        wwa = wwa.at[0:DECAY_LORA, 0:RWKV_W].set(rw_w_up[l].astype(F32))
        wwa = wwa.at[DECAY_LORA:PAIR, RWKV_W:2 * RWKV_W].set(rw_a_up[l].astype(F32))
        wwa_hi, wwa_lo = _hi_lo(wwa)
        retv = jnp.concatenate([jnp.stack([ret_gn_g[l].astype(F32), ret_gn_b[l].astype(F32), chunk_decay]),
                                jnp.zeros((5, RET_W), F32)])
        mix_params = (row2(rw_mu[l]), rwv, wwa_hi, wwa_lo, rw_g_up[l].astype(BF16), retv,
                      _attention_bias(attn_rel_bias[l]))
        y = _mixers(z.reshape(batch, seq, N_IN_COLS), mix_params, consts, batch, seq)

        xcur = _out_ffn(y, xcur.reshape(batch, seq, D_MODEL), w_out[l].astype(BF16),
                        row2(ln1_g[l]), row2(ln1_b[l]), ffn_w_up[l].astype(BF16),
                        ffn_conv_w[l].astype(F32), row2(ffn_conv_b[l]), ffn_w_down[l].astype(BF16),
                        row2(ln2_g[l]), row2(ln2_b[l])).reshape(batch * seq, D_MODEL)
    return xcur.reshape(batch, seq, D_MODEL)
```

```python
import functools

import jax
import jax.numpy as jnp
from jax import lax
from jax.experimental import pallas as pl
from jax.experimental.pallas import tpu as pltpu

F32 = jnp.float32
BF16 = jnp.bfloat16

D_MODEL = 1024
DEPTH = 2
CHUNK = 64
HEAD_DIM = 64
PAIR = 2 * HEAD_DIM
RWKV_HEADS, RET_HEADS, ATT_HEADS = 6, 6, 4
RWKV_W, RET_W, ATT_W = 384, 384, 256
DECAY_LORA, AAA_LORA, GATE_LORA = 64, 64, 128
RWKV_COLS = 3 * RWKV_W + DECAY_LORA + AAA_LORA + GATE_LORA
RET_COLS = 4 * RET_W
ATT_COLS = 3 * ATT_W
N_IN_COLS = RWKV_COLS + RET_COLS + ATT_COLS
RET_OFF = RWKV_COLS
ATT_OFF = RWKV_COLS + RET_COLS
BAND_PREV_CHUNKS = 8
BAND_PAD = BAND_PREV_CHUNKS * CHUNK
BAND = BAND_PAD + CHUNK
REL_CLIP = 128
D_FF = 2816
CONV_W = 3
ALPHA = (2 * DEPTH) ** 0.25
ROPE_BASE = 10000.0
LN_EPS = 1e-5
RWKV_GN_EPS = 64e-5
RET_GN_EPS = 1e-5
DECAY_SCALE = -0.6065306597126334

ROW_TILE = 512
FFN_COLS = 256
FFN_SUB = 512
FFN_LOOKAHEAD = 2
FFN_SLOTS = FFN_LOOKAHEAD + 1
MIX_BATCH = 8
PREP_FILL = 1
ATT_LAG = 2
VMEM_LIMIT = 56 * 1024 * 1024

assert CHUNK == HEAD_DIM


def _dot(a, b):
    return jnp.dot(a.astype(BF16), b.astype(BF16), preferred_element_type=F32)


def _dot_nt(a, b):
    return lax.dot_general(a.astype(BF16), b.astype(BF16), (((1,), (1,)), ((), ())),
                           preferred_element_type=F32)


def _dot_tn(a, b):
    return lax.dot_general(a.astype(BF16), b.astype(BF16), (((0,), (0,)), ((), ())),
                           preferred_element_type=F32)


def _split2(x):
    hi = x.astype(BF16)
    lo = (x - hi.astype(F32)).astype(BF16)
    return hi, lo


def _split3(x):
    hi = x.astype(BF16)
    r1 = x - hi.astype(F32)
    mid = r1.astype(BF16)
    lo = (r1 - mid.astype(F32)).astype(BF16)
    return hi, mid, lo


def _dot_x3(x, w_hi, w_lo):
    x_hi, x_lo = _split2(x)
    return (jnp.dot(x_hi, w_hi, preferred_element_type=F32)
            + jnp.dot(x_hi, w_lo, preferred_element_type=F32)
            + jnp.dot(x_lo, w_hi, preferred_element_type=F32))


def _layer_norm(x, g, b, eps=LN_EPS):
    mu = jnp.mean(x, axis=-1, keepdims=True)
    d = x - mu
    var = jnp.mean(d * d, axis=-1, keepdims=True)
    return d * lax.rsqrt(var + eps) * g + b


def _sigmoid(x):
    return 1.0 / (1.0 + jnp.exp(-x))


def _proj_kernel(apply_ln, tiles_per_seq, x_ref, g_ref, b_ref, w_ref, mu_ref, z_ref, *rest):
    carry_ref = rest[-1]
    tm = x_ref.shape[0]

    @pl.when(pl.program_id(0) % tiles_per_seq == 0)
    def _():
        carry_ref[...] = jnp.zeros_like(carry_ref)

    x = x_ref[...]
    if apply_ln:
        x = _layer_norm(x, g_ref[...], b_ref[...])
        rest[0][...] = x
    xb = x.astype(BF16)
    z_ref[:, RWKV_COLS:N_IN_COLS] = jnp.dot(xb, w_ref[:, RWKV_COLS:N_IN_COLS], preferred_element_type=F32)
    zr = jnp.dot(xb, w_ref[:, 0:RWKV_COLS], preferred_element_type=F32)
    row8 = lax.broadcasted_iota(jnp.int32, (8, RWKV_COLS), 0)
    rolled = pltpu.roll(zr, 1, 0)
    prev = jnp.concatenate([jnp.where(row8 == 0, carry_ref[7:8, :], rolled[0:8]), rolled[8:tm]], axis=0)
    carry_ref[...] = zr[tm - 8:tm, :]
    z_ref[:, 0:RWKV_COLS] = zr + (prev - zr) * mu_ref[...]


def _project(x2d, ln_g, ln_b, w_bf16, mu, apply_ln, seq):
    m = x2d.shape[0]
    tm = ROW_TILE
    row = lambda i: (i, 0)
    const = lambda i: (0, 0)
    out_shape = [jax.ShapeDtypeStruct((m, N_IN_COLS), F32)]
    out_specs = [pl.BlockSpec((tm, N_IN_COLS), row)]
    if apply_ln:
        out_shape.append(jax.ShapeDtypeStruct((m, D_MODEL), F32))
        out_specs.append(pl.BlockSpec((tm, D_MODEL), row))
    outs = pl.pallas_call(
        functools.partial(_proj_kernel, apply_ln, seq // tm),
        grid=(m // tm,),
        in_specs=[pl.BlockSpec((tm, D_MODEL), row),
                  pl.BlockSpec((1, D_MODEL), const),
                  pl.BlockSpec((1, D_MODEL), const),
                  pl.BlockSpec((D_MODEL, N_IN_COLS), const, pipeline_mode=pl.Buffered(1)),
                  pl.BlockSpec((1, RWKV_COLS), const)],
        out_specs=out_specs,
        out_shape=out_shape,
        scratch_shapes=[pltpu.VMEM((8, RWKV_COLS), F32)],
        compiler_params=pltpu.CompilerParams(dimension_semantics=("arbitrary",),
                                             vmem_limit_bytes=VMEM_LIMIT),
        name="in_proj",
    )(x2d, ln_g, ln_b, w_bf16, mu)
    return outs


def _mixer_kernel(z_ref, rwv_ref, wwa_hi_ref, wwa_lo_ref, gup_ref,
                  retv_ref, cos_ref, sin_ref, dmat_ref, qdec_ref, kdec_ref, bias_ref,
                  y_ref, s_ref, r_ref, kbuf_ref, vbuf_ref):
    n = pl.program_id(1)
    C = CHUNK

    @pl.when(n == 0)
    def _():
        s_ref[...] = jnp.zeros_like(s_ref)
        r_ref[...] = jnp.zeros_like(r_ref)
        kbuf_ref[:, :, 0:BAND_PAD, :] = jnp.zeros((MIX_BATCH, ATT_HEADS // 2, BAND_PAD, PAIR), BF16)
        vbuf_ref[:, :, 0:BAND_PAD, :] = jnp.zeros((MIX_BATCH, ATT_HEADS // 2, BAND_PAD, PAIR), BF16)

    lane = lax.broadcasted_iota(jnp.int32, (C, PAIR), 1)
    time = lax.broadcasted_iota(jnp.int32, (C, PAIR), 0)
    head0 = lane < HEAD_DIM
    cat_incl = time >= (lane % C)
    cat_strict = time > (lane % C)

    def blk(x):
        zero = jnp.zeros_like(x)
        return jnp.concatenate([jnp.where(head0, x, zero), jnp.where(head0, zero, x)], axis=0)

    def zcols(i, start, width):
        return z_ref[i, :, start:start + width]

    r2 = lax.broadcasted_iota(jnp.int32, (2 * PAIR, 2 * PAIR), 0)
    c2 = lax.broadcasted_iota(jnp.int32, (2 * PAIR, 2 * PAIR), 1)
    head_ones = ((r2 // HEAD_DIM) == (c2 // HEAD_DIM)).astype(BF16)

    def head_sums(slabs):
        x = jnp.concatenate([jnp.concatenate(slabs[i:i + 2], axis=1) for i in range(0, len(slabs), 2)], axis=0)
        s = jnp.dot(x.astype(BF16), head_ones, preferred_element_type=F32)
        return [s[(i // 2) * C:(i // 2 + 1) * C, (i % 2) * PAIR:(i % 2 + 1) * PAIR] for i in range(len(slabs))]

    def head_norms(ys, gs, bs, epss):
        mus = head_sums(ys)
        ds = [y - mu * (1.0 / HEAD_DIM) for y, mu in zip(ys, mus)]
        vs = head_sums([d * d for d in ds])
        return [d * lax.rsqrt(var * (1.0 / HEAD_DIM) + eps) * g + b
                for d, var, g, b, eps in zip(ds, vs, gs, bs, epss)]

    w0 = rwv_ref[0:1, :]
    a0 = rwv_ref[1:2, :]
    k_k = rwv_ref[2:3, :]
    k_a = rwv_ref[3:4, :]
    r_k = rwv_ref[4:5, :]
    ln_g = rwv_ref[5:6, :]
    ln_b = rwv_ref[6:7, :]
    gn_g = retv_ref[0:1, :]
    gn_b = retv_ref[1:2, :]
    chunk_decay = retv_ref[2:3, :]
    cos_t = cos_ref[...]
    sin_t = sin_ref[...]
    first_half = (lane % HEAD_DIM) < (HEAD_DIM // 2)
    tri = (lax.broadcasted_iota(jnp.int32, (C, C), 0)
           >= lax.broadcasted_iota(jnp.int32, (C, C), 1)).astype(BF16)
    off = pl.multiple_of(n * C, C)
    col = lax.broadcasted_iota(jnp.int32, (PAIR, BAND), 1)
    valid = col >= BAND_PAD - n * C
    pairs = [slice(p * PAIR, (p + 1) * PAIR) for p in range(RWKV_HEADS // 2)]

    def rope(t):
        partner = jnp.where(first_half, pltpu.roll(t, PAIR - HEAD_DIM // 2, 1),
                            pltpu.roll(t, HEAD_DIM // 2, 1))
        return t * cos_t + partner * sin_t

    def prepare(i, sink):
        zs = zcols(i, 0, RWKV_COLS)
        r = zs[:, 0:RWKV_W]
        k = zs[:, RWKV_W:2 * RWKV_W]
        v = zs[:, 2 * RWKV_W:3 * RWKV_W]
        wa_l = zs[:, 3 * RWKV_W:3 * RWKV_W + PAIR]
        g_l = zs[:, 3 * RWKV_W + PAIR:RWKV_COLS]

        wa_in = jnp.where(head0, jnp.tanh(wa_l), wa_l)
        lora = _dot_x3(wa_in, wwa_hi_ref[...], wwa_lo_ref[...])
        w_pre = w0 + lora[:, 0:RWKV_W]
        a = _sigmoid(a0 + lora[:, RWKV_W:2 * RWKV_W])
        g = jnp.dot(_sigmoid(g_l).astype(BF16), gup_ref[...], preferred_element_type=F32)
        log_decay = DECAY_SCALE * _sigmoid(w_pre)
        yield

        ld2 = jnp.concatenate(_split2(log_decay), axis=1)
        cs = jnp.dot(tri, ld2, preferred_element_type=F32)
        cum = cs[:, 0:RWKV_W] + cs[:, RWKV_W:2 * RWKV_W]
        e_incl = jnp.exp(cum)
        e_excl = jnp.exp(cum - log_decay)
        e_inv = jnp.exp(-cum)
        yield

        kk = k * k_k
        k2 = k * (1.0 + (a - 1.0) * k_a)
        rkr = r * k2 * r_k
        sums = head_sums([kk[:, sl] * kk[:, sl] for sl in pairs] + [rkr[:, sl] for sl in pairs])
        kk_ss, rkr_sum = sums[0:3], sums[3:6]
        sink["vals"] = dict(v=v, g=g, rkr_sum=rkr_sum)
        sink["chains"] = chains = []
        yield

        for p, sl in enumerate(pairs):
            kk_p = kk[:, sl] / jnp.maximum(jnp.sqrt(kk_ss[p]), 1e-12)
            a_hat = -kk_p * e_excl[:, sl]
            r_hat = r[:, sl] * e_incl[:, sl]
            b_til = blk(kk_p * a[:, sl] * e_inv[:, sl])
            k_til = blk(k2[:, sl] * e_inv[:, sl])
            chains.append(dict(
                i=i, p=p,
                ar=jnp.concatenate([a_hat, r_hat], axis=0).astype(BF16),
                bk=jnp.concatenate([b_til, k_til], axis=0).astype(BF16),
                v_b=blk(v[:, sl]).astype(BF16),
                p_end=e_incl[C - 1:C, sl]))
            yield

    y_ret = {}
    gates = {}
    y_att = {}

    def retention_tasks(i, p):
        st = {}
        sl = pairs[p]
        base = RET_OFF + p * PAIR

        def scores():
            q = rope(zcols(i, base, PAIR))
            kr = rope(zcols(i, base + RET_W, PAIR)) * (HEAD_DIM ** -0.5)
            gg = zcols(i, base + 3 * RET_W, PAIR)
            gates[i, p] = gg * _sigmoid(gg)
            st["q_dec"] = q * qdec_ref[:, sl]
            st["kd_b"] = blk(kr * kdec_ref[:, sl])
            st["v_b"] = blk(z_ref[i, :, base + 2 * RET_W:base + 2 * RET_W + PAIR].astype(BF16))
            st["scores"] = _dot_nt(q, blk(kr)) * dmat_ref[p]

        def output():
            r_old = r_ref[i, p]
            lhs = jnp.concatenate([st["scores"], st["q_dec"]], axis=1)
            rhs = jnp.concatenate([st["v_b"], r_old.astype(BF16)], axis=0)
            y_ret[i, p] = _dot(lhs, rhs)
            r_ref[i, p] = r_old * chunk_decay[:, sl] + _dot_tn(st["kd_b"], st["v_b"])

        return [scores, output]

    def attention_tasks(i, p):
        st = {}
        base = ATT_OFF + p * PAIR

        def scores():
            q = zcols(i, base, PAIR) * (HEAD_DIM ** -0.5)
            kbuf_ref[i, p, pl.ds(BAND_PAD + off, C), :] = (
                z_ref[i, :, base + ATT_W:base + ATT_W + PAIR].astype(BF16))
            vbuf_ref[i, p, pl.ds(BAND_PAD + off, C), :] = (
                z_ref[i, :, base + 2 * ATT_W:base + 2 * ATT_W + PAIR].astype(BF16))
            k_band = kbuf_ref[i, p, pl.ds(off, BAND), :]
            s = _dot_nt(blk(q), k_band) + bias_ref[p]
            s = jnp.where(valid, s, -1e30)
            m = jnp.max(s, axis=-1, keepdims=True)
            st["e"] = jnp.exp(s - m)
            st["l"] = jnp.sum(st["e"], axis=-1, keepdims=True)

        def output():
            v_band = vbuf_ref[i, p, pl.ds(off, BAND), :]
            o = _dot(st["e"], v_band) / st["l"]
            y_att[i, p] = jnp.where(head0, o[0:C], o[C:2 * C])

        return [scores, output]

    y_rwkv = {}

    def chain_stages(chains):
        for c in chains:
            c["s_old"] = s_ref[c["i"], c["p"]]
            big = _dot_nt(c["ar"], c["bk"])
            c["pw"] = jnp.where(cat_strict, big[0:C, 0:PAIR], 0.0).astype(BF16)
            c["a_ak"] = jnp.where(cat_strict, big[0:C, PAIR:2 * PAIR], 0.0)
            c["l_rbk"] = jnp.concatenate(
                [jnp.where(cat_incl, big[C:2 * C, 0:PAIR], 0.0),
                 jnp.where(cat_incl, big[C:2 * C, PAIR:2 * PAIR], 0.0)], axis=1).astype(BF16)
        yield
        for c in chains:
            c["ars"] = _dot_nt(c["ar"], c["s_old"])
        yield
        for c in chains:
            c["x"] = c["ars"][0:C] + _dot(c["a_ak"], c["v_b"])
        yield
        for _ in range(5):
            for c in chains:
                rhs = jnp.concatenate([blk(c["x"]).astype(BF16), blk(c["pw"])], axis=1)
                res = jnp.dot(c["pw"], rhs, preferred_element_type=F32)
                c["x"] = c["x"] + res[:, 0:PAIR]
                c["pw"] = res[:, PAIR:2 * PAIR].astype(BF16)
            yield
        for c in chains:
            c["x"] = c["x"] + jnp.dot(c["pw"], blk(c["x"]).astype(BF16), preferred_element_type=F32)
            c["uv"] = jnp.concatenate([blk(c["x"]).astype(BF16), c["v_b"]], axis=0)
        yield
        for c in chains:
            y_rwkv[c["i"], c["p"]] = c["ars"][C:2 * C] + jnp.dot(c["l_rbk"], c["uv"], preferred_element_type=F32)
        yield
        for c in chains:
            s_ref[c["i"], c["p"]] = (c["s_old"] + _dot_tn(c["uv"], c["bk"])) * c["p_end"]
        yield

    seqs = range(MIX_BATCH)
    ret_tasks = [retention_tasks(i, p) for i in seqs for p in range(RET_HEADS // 2)]
    att_tasks = [attention_tasks(i, p) for i in seqs for p in range(ATT_HEADS // 2)]

    def run(tasks, count):
        for _ in range(count):
            if tasks:
                tasks.pop(0)()

    ret_queue = [t[0] for t in ret_tasks] + [t[1] for t in ret_tasks]
    sinks = {i: {} for i in seqs}
    for i in seqs:
        for _ in prepare(i, sinks[i]):
            run(ret_queue, PREP_FILL)
    run(ret_queue, len(ret_queue))
    att_scores = [t[0] for t in att_tasks]
    att_outputs = [t[1] for t in att_tasks]
    for stage, _ in enumerate(chain_stages([c for i in seqs for c in sinks[i]["chains"]])):
        run(att_scores, 1)
        if stage >= ATT_LAG:
            run(att_outputs, 1)
    run(att_scores, len(att_scores))
    run(att_outputs, len(att_outputs))
    seq_vals = {i: sinks[i]["vals"] for i in seqs}

    normed_all = head_norms(
        [y for i in seqs for y in ([y_rwkv[i, p] for p in range(3)] + [y_ret[i, p] for p in range(3)])],
        ([ln_g[:, sl] for sl in pairs] + [gn_g[:, sl] for sl in pairs]) * MIX_BATCH,
        ([ln_b[:, sl] for sl in pairs] + [gn_b[:, sl] for sl in pairs]) * MIX_BATCH,
        ([RWKV_GN_EPS] * 3 + [RET_GN_EPS] * 3) * MIX_BATCH)
    for i in seqs:
        vals = seq_vals[i]
        normed = normed_all[6 * i:6 * i + 6]
        out = []
        for p, sl in enumerate(pairs):
            out.append((normed[p] + vals["rkr_sum"][p] * vals["v"][:, sl]) * vals["g"][:, sl])
        for p in range(RET_HEADS // 2):
            out.append(normed[3 + p] * gates[i, p])
        for p in range(ATT_HEADS // 2):
            out.append(y_att[i, p])
        y_ref[i] = jnp.concatenate(out, axis=1)


def _mixers(z, mix_params, consts, batch, seq):
    nc = seq // CHUNK
    nb = MIX_BATCH
    c2 = lambda b, n: (0, 0)
    c3 = lambda b, n: (0, 0, 0)
    (rwv, wwa_hi, wwa_lo, gup, retv, bias) = mix_params
    (cos_t, sin_t, dmat, qdec, kdec) = consts
    in_specs = [
        pl.BlockSpec((nb, CHUNK, N_IN_COLS), lambda b, n: (b, n, 0)),
        pl.BlockSpec((8, RWKV_W), c2),
        pl.BlockSpec((PAIR, 2 * RWKV_W), c2),
        pl.BlockSpec((PAIR, 2 * RWKV_W), c2),
        pl.BlockSpec((GATE_LORA, RWKV_W), c2),
        pl.BlockSpec((8, RET_W), c2),
        pl.BlockSpec((CHUNK, PAIR), lambda b, n: (n, 0)),
        pl.BlockSpec((CHUNK, PAIR), lambda b, n: (n, 0)),
        pl.BlockSpec((RET_HEADS // 2, CHUNK, PAIR), c3),
        pl.BlockSpec((CHUNK, RET_W), c2),
        pl.BlockSpec((CHUNK, RET_W), c2),
        pl.BlockSpec((ATT_HEADS // 2, PAIR, BAND), c3),
    ]
    return pl.pallas_call(
        _mixer_kernel,
        grid=(batch // nb, nc),
        in_specs=in_specs,
        out_specs=pl.BlockSpec((nb, CHUNK, D_MODEL), lambda b, n: (b, n, 0)),
        out_shape=jax.ShapeDtypeStruct((batch, seq, D_MODEL), F32),
        scratch_shapes=[
            pltpu.VMEM((nb, RWKV_HEADS // 2, PAIR, PAIR), F32),
            pltpu.VMEM((nb, RET_HEADS // 2, PAIR, PAIR), F32),
            pltpu.VMEM((nb, ATT_HEADS // 2, BAND_PAD + seq, PAIR), BF16),
            pltpu.VMEM((nb, ATT_HEADS // 2, BAND_PAD + seq, PAIR), BF16),
        ],
        compiler_params=pltpu.CompilerParams(dimension_semantics=("parallel", "arbitrary"),
                                             vmem_limit_bytes=VMEM_LIMIT),
        name="mixers",
    )(z, rwv, wwa_hi, wwa_lo, gup, retv, cos_t, sin_t, dmat, qdec, kdec, bias)


def _ffn_kernel(y_ref, x_ref, wout_ref, ln1g_ref, ln1b_ref, wup_ref, cw_ref, cb_ref, wdn_ref,
                ln2g_ref, ln2b_ref, o_ref, carry_ref, acc_ref, x1_ref, *slot_refs):
    tm = x_ref.shape[0]
    sub = min(FFN_SUB, tm)
    n_sub = tm // sub
    n_blocks = D_FF // FFN_COLS
    bw = 2 * FFN_COLS

    @pl.when(pl.program_id(1) == 0)
    def _():
        carry_ref[...] = jnp.zeros_like(carry_ref)

    u_refs = slot_refs[0:FFN_SLOTS]
    h_refs = slot_refs[FFN_SLOTS:2 * FFN_SLOTS]

    def rows_of(t):
        r = t // n_blocks
        return slice(r * sub, (r + 1) * sub)

    def head(r):
        rows = slice(r * sub, (r + 1) * sub)
        h = ALPHA * x_ref[rows, :] + jnp.dot(y_ref[rows, :].astype(BF16), wout_ref[...],
                                              preferred_element_type=F32)
        x1 = _layer_norm(h, ln1g_ref[...], ln1b_ref[...])
        x1_ref[rows, :] = x1.astype(BF16)
        acc_ref[rows, :] = ALPHA * x1

    def up(t):
        j = t % n_blocks
        u_ref = u_refs[t % FFN_SLOTS]
        cols = slice(j * bw, (j + 1) * bw)
        gcols = slice(j * FFN_COLS, (j + 1) * FFN_COLS)
        vcols = slice(D_FF + j * FFN_COLS, D_FF + (j + 1) * FFN_COLS)
        x1b = x1_ref[rows_of(t), :]
        u_ref[0:8, :] = carry_ref[:, cols]
        u_ref[8:8 + sub, 0:FFN_COLS] = jnp.dot(x1b, wup_ref[:, gcols], preferred_element_type=F32)
        u_ref[8:8 + sub, FFN_COLS:bw] = jnp.dot(x1b, wup_ref[:, vcols], preferred_element_type=F32)
        carry_ref[:, cols] = u_ref[sub:sub + 8, :]

    def gated(t):
        j = t % n_blocks
        u_ref = u_refs[t % FFN_SLOTS]
        halves = []
        for part in range(2):
            cols = slice(part * D_FF + j * FFN_COLS, part * D_FF + (j + 1) * FFN_COLS)
            ucols = slice(part * FFN_COLS, (part + 1) * FFN_COLS)
            halves.append(cb_ref[:, cols] + cw_ref[0:1, cols] * u_ref[6:6 + sub, ucols]
                          + cw_ref[1:2, cols] * u_ref[7:7 + sub, ucols]
                          + cw_ref[2:3, cols] * u_ref[8:8 + sub, ucols])
        gate, val = halves
        h_refs[t % FFN_SLOTS][...] = (gate * _sigmoid(gate) * val).astype(BF16)

    def down(t):
        j = t % n_blocks
        acc_ref[rows_of(t), :] += jnp.dot(h_refs[t % FFN_SLOTS][...], wdn_ref[j * FFN_COLS:(j + 1) * FFN_COLS, :],
                                          preferred_element_type=F32)

    def tail(r):
        rows = slice(r * sub, (r + 1) * sub)
        o_ref[rows, :] = _layer_norm(acc_ref[rows, :], ln2g_ref[...], ln2b_ref[...])

    total = n_sub * n_blocks
    head(0)
    for t in range(min(FFN_LOOKAHEAD, total)):
        up(t)
    for t in range(total + 1):
        nxt = t + FFN_LOOKAHEAD + 1
        if nxt < total and nxt % n_blocks == 0:
            head(nxt // n_blocks)
        if t + FFN_LOOKAHEAD < total:
            up(t + FFN_LOOKAHEAD)
        if t < total:
            gated(t)
        if t >= 1:
            down(t - 1)
            if t % n_blocks == 0:
                tail(t // n_blocks - 1)


def _out_ffn(y, x, wout, ln1g, ln1b, wup, cw, cb, wdn, ln2g, ln2b):
    batch, seq, _ = x.shape
    tm = min(ROW_TILE, seq)
    rows = lambda b, s: (b, s, 0)
    c2 = lambda b, s: (0, 0)
    one = pl.Buffered(1)
    return pl.pallas_call(
        _ffn_kernel,
        grid=(batch, seq // tm),
        in_specs=[
            pl.BlockSpec((None, tm, D_MODEL), rows),
            pl.BlockSpec((None, tm, D_MODEL), rows),
            pl.BlockSpec((D_MODEL, D_MODEL), c2, pipeline_mode=one),
            pl.BlockSpec((1, D_MODEL), c2),
            pl.BlockSpec((1, D_MODEL), c2),
            pl.BlockSpec((D_MODEL, 2 * D_FF), c2, pipeline_mode=one),
            pl.BlockSpec((CONV_W, 2 * D_FF), c2),
            pl.BlockSpec((1, 2 * D_FF), c2),
            pl.BlockSpec((D_FF, D_MODEL), c2, pipeline_mode=one),
            pl.BlockSpec((1, D_MODEL), c2),
            pl.BlockSpec((1, D_MODEL), c2),
        ],
        out_specs=pl.BlockSpec((None, tm, D_MODEL), rows),
        out_shape=jax.ShapeDtypeStruct((batch, seq, D_MODEL), F32),
        scratch_shapes=[pltpu.VMEM((8, 2 * D_FF), F32),
                        pltpu.VMEM((tm, D_MODEL), F32),
                        pltpu.VMEM((tm, D_MODEL), BF16),
                        *[pltpu.VMEM((min(FFN_SUB, tm) + 8, 2 * FFN_COLS), F32) for _ in range(FFN_SLOTS)],
                        *[pltpu.VMEM((min(FFN_SUB, tm), FFN_COLS), BF16) for _ in range(FFN_SLOTS)]],
        compiler_params=pltpu.CompilerParams(dimension_semantics=("parallel", "arbitrary"),
                                             vmem_limit_bytes=VMEM_LIMIT),
        name="out_ffn",
    )(y, x, wout, ln1g, ln1b, wup, cw, cb, wdn, ln2g, ln2b)


def _hi_lo(w):
    hi = w.astype(BF16)
    return hi, (w - hi.astype(F32)).astype(BF16)


def _mixer_constants(seq):
    half = HEAD_DIM // 2
    lane = jnp.arange(PAIR)
    inv = ROPE_BASE ** (-jnp.arange(half, dtype=F32) / half)
    ang = jnp.arange(seq, dtype=F32)[:, None] * inv[None, :]
    idx = lane % half
    cos_t = jnp.cos(ang)[:, idx]
    sign = jnp.where((lane % HEAD_DIM) < half, -1.0, 1.0).astype(F32)
    sin_t = jnp.sin(ang)[:, idx] * sign[None, :]

    log_g = jnp.log(1.0 - jnp.exp2(-5.0 - jnp.arange(RET_HEADS, dtype=F32)))
    t = jnp.arange(CHUNK, dtype=F32)
    lane_s = (lane % CHUNK).astype(F32)
    lane_h = lane // CHUNK
    pairs = jnp.arange(RET_HEADS // 2)
    lg_cat = log_g[2 * pairs[:, None] + lane_h[None, :]]
    dist = jnp.abs(t[:, None] - lane_s[None, :])
    dmat = jnp.exp(lg_cat[:, None, :] * dist[None])
    lg_lane = jnp.repeat(log_g, HEAD_DIM)
    qdec = jnp.exp(lg_lane[None, :] * (t + 1.0)[:, None])
    kdec = jnp.exp(lg_lane[None, :] * (CHUNK - 1.0 - t)[:, None])
    chunk_decay = jnp.exp(lg_lane * CHUNK)
    return (cos_t, sin_t, dmat.astype(F32), qdec.astype(F32), kdec.astype(F32)), chunk_decay


def _attention_bias(rel_bias):
    t = jnp.arange(BAND + CHUNK - 1)
    ext_idx = jnp.clip(BAND_PAD + (CHUNK - 1) - t, -(CHUNK - 1), REL_CLIP) + (CHUNK - 1)
    ext = rel_bias.astype(F32)[:, ext_idx]
    bias = jnp.stack([ext[:, CHUNK - 1 - i:CHUNK - 1 - i + BAND] for i in range(CHUNK)], axis=1)
    return bias.reshape(ATT_HEADS // 2, PAIR, BAND)


def kernel(x, ln_in_g, ln_in_b, w_in, rw_mu, rw_w0, rw_w_up, rw_a0, rw_a_up, rw_g_up, rw_k_k, rw_k_a, rw_r_k, rw_ln_g, rw_ln_b, ret_gn_g, ret_gn_b, attn_rel_bias, w_out, ln1_g, ln1_b, ffn_w_up, ffn_conv_w, ffn_conv_b, ffn_w_down, ln2_g, ln2_b):
    batch, seq, _ = x.shape
    consts, chunk_decay = _mixer_constants(seq)
    row2 = lambda t: t.reshape(1, -1).astype(F32)
    xcur = x.reshape(batch * seq, D_MODEL)
    for l in range(DEPTH):
        if l == 0:
            z, xcur = _project(xcur, row2(ln_in_g), row2(ln_in_b), w_in[l].astype(BF16), row2(rw_mu[l]), True, seq)
        else:
            (z,) = _project(xcur, row2(ln_in_g), row2(ln_in_b), w_in[l].astype(BF16), row2(rw_mu[l]), False, seq)

        zero_row = jnp.zeros((RWKV_W,), F32)
        rwv = jnp.stack([rw_w0[l], rw_a0[l], rw_k_k[l], rw_k_a[l], rw_r_k[l].reshape(-1),
                         rw_ln_g[l], rw_ln_b[l], zero_row]).astype(F32)
        wwa = jnp.zeros((PAIR, 2 * RWKV_W), F32)
        wwa = wwa.at[0:DECAY_LORA, 0:RWKV_W].set(rw_w_up[l].astype(F32))
        wwa = wwa.at[DECAY_LORA:PAIR, RWKV_W:2 * RWKV_W].set(rw_a_up[l].astype(F32))
        wwa_hi, wwa_lo = _hi_lo(wwa)
        retv = jnp.concatenate([jnp.stack([ret_gn_g[l].astype(F32), ret_gn_b[l].astype(F32), chunk_decay]),
                                jnp.zeros((5, RET_W), F32)])
        mix_params = (rwv, wwa_hi, wwa_lo, rw_g_up[l].astype(BF16), retv,
                      _attention_bias(attn_rel_bias[l]))
        y = _mixers(z.reshape(batch, seq, N_IN_COLS), mix_params, consts, batch, seq)

        xcur = _out_ffn(y, xcur.reshape(batch, seq, D_MODEL), w_out[l].astype(BF16),
                        row2(ln1_g[l]), row2(ln1_b[l]), ffn_w_up[l].astype(BF16),
                        ffn_conv_w[l].astype(F32), row2(ffn_conv_b[l]), ffn_w_down[l].astype(BF16),
                        row2(ln2_g[l]), row2(ln2_b[l])).reshape(batch * seq, D_MODEL)
    return xcur.reshape(batch, seq, D_MODEL)
```
